```python
import math
import jax, jax.numpy as jnp
from jax import lax
import numpy as np

D_MODEL = 1024
BATCH = 4
SEQ = 4096
DEPTH = 4
DEC_BATCH = 32
DEC_SEQ = 64
PAST_LEN = 2048

CHUNK = 64
Q_BLOCK = 128
N_MEM = 256
D_FF = 4096
EPS = 1e-6

SSM_WIDTH = 256
SSM_GROUP = 16
SSM_GROUPS = SSM_WIDTH // SSM_GROUP
SSM_STATE = 64
SSM_DT_MIN = 1e-3
SSM_DT_MAX = 1e-1
ATT_HEADS = 4
ATT_HEAD_DIM = 64
ATT_HALF = ATT_HEAD_DIM // 2
ATT_WIDTH = ATT_HEADS * ATT_HEAD_DIM
CONV_WIDTH = 256
CONV_K = 31
GMLP_WIDTH = 256
GMLP_HEADS = 4
GMLP_HEAD_DIM = GMLP_WIDTH // GMLP_HEADS
GMLP_CHUNK = 128
X_HEADS = 4
X_HEAD_DIM = 128
X_WIDTH = X_HEADS * X_HEAD_DIM

OFF_SSM = 0
OFF_Q = OFF_SSM + SSM_WIDTH
OFF_K = OFF_Q + ATT_WIDTH
OFF_V = OFF_K + ATT_WIDTH
OFF_CONV = OFF_V + ATT_WIDTH
OFF_GMLP = OFF_CONV + 2 * CONV_WIDTH
IN_WIDTH = OFF_GMLP + 2 * GMLP_WIDTH
MIX_WIDTH = SSM_WIDTH + ATT_WIDTH + CONV_WIDTH + GMLP_WIDTH

kernel_name = 'hybrid_streaming_encoder_step'


def rms_norm(x, g):
    xf = x.astype(jnp.float32)
    y = xf * lax.rsqrt(jnp.mean(xf * xf, axis=-1, keepdims=True) + EPS)
    return (y * g.astype(jnp.float32)).astype(x.dtype)


def layer_norm(x, g, b):
    xf = x.astype(jnp.float32)
    mu = jnp.mean(xf, axis=-1, keepdims=True)
    var = jnp.mean(jnp.square(xf - mu), axis=-1, keepdims=True)
    y = (xf - mu) * lax.rsqrt(var + EPS) * g.astype(jnp.float32) + b.astype(jnp.float32)
    return y.astype(x.dtype)


def swiglu_ffn(h, w_gate, w_up, w_down):
    return (jax.nn.silu(h @ w_gate) * (h @ w_up)) @ w_down


def s5_mixer(u, s0_re, s0_im, a_re, a_im, b_re, b_im, c_re, c_im, d, log_dt, w_glu, b_glu):
    f32 = jnp.float32
    bt, t, _ = u.shape
    ug = u.astype(f32).reshape(bt, t, SSM_GROUPS, SSM_GROUP)
    lam = lax.complex(a_re.astype(f32), a_im.astype(f32))
    dt = jnp.exp(log_dt.astype(f32))[:, None]
    a_bar = jnp.exp(lam * dt)
    b_c = lax.complex(b_re.astype(f32), b_im.astype(f32))
    b_bar = ((a_bar - 1.0) / lam)[..., None] * b_c
    bu = jnp.einsum('btgc,gpc->btgp', ug.astype(jnp.complex64), b_bar)
    a_seq = jnp.broadcast_to(a_bar, bu.shape)

    def combine(e1, e2):
        return e1[0] * e2[0], e2[0] * e1[1] + e2[1]

    a_cum, s = lax.associative_scan(combine, (a_seq, bu), axis=1)
    s0 = lax.complex(s0_re.astype(f32), s0_im.astype(f32))
    s = s + a_cum * s0[:, None]
    c_c = lax.complex(c_re.astype(f32), c_im.astype(f32))
    y = jnp.real(jnp.einsum('btgp,gcp->btgc', s, c_c)) + d.astype(f32) * ug
    y = y.reshape(bt, t, SSM_WIDTH)
    g = jax.nn.gelu(y)
    out = g * jax.nn.sigmoid(g @ w_glu.astype(f32) + b_glu.astype(f32))
    s_last = s[:, -1]
    return out.astype(u.dtype), jnp.real(s_last).astype(s0_re.dtype), jnp.imag(s_last).astype(s0_re.dtype)


def alibi_slopes():
    return 2.0 ** (-8.0 * jnp.arange(1, ATT_HEADS + 1, dtype=jnp.float32) / ATT_HEADS)


def diff_attention(q, k, v, q_pos, k_pos, lam, lam_init, g_head):
    f32 = jnp.float32
    bt, tq = q.shape[0], q.shape[1]
    blk = min(tq, Q_BLOCK)
    nblk = tq // blk
    kf = k.astype(f32)
    vf = v.astype(f32)
    k1, k2 = kf[..., :ATT_HALF], kf[..., ATT_HALF:]
    slopes = alibi_slopes()
    k_chunk = k_pos // CHUNK
    scale = ATT_HALF ** -0.5

    def block(args):
        qb, qp = args
        qb = qb.astype(f32) * scale
        dist = jnp.abs(qp[:, None] - k_pos[None, :]).astype(f32)
        visible = k_chunk[None, :] <= (qp // CHUNK)[:, None]
        bias = jnp.where(visible[None], -slopes[:, None, None] * dist[None], -jnp.inf)
        a1 = jax.nn.softmax(jnp.einsum('bqhd,bkhd->bhqk', qb[..., :ATT_HALF], k1) + bias, axis=-1)
        a2 = jax.nn.softmax(jnp.einsum('bqhd,bkhd->bhqk', qb[..., ATT_HALF:], k2) + bias, axis=-1)
        return jnp.einsum('bhqk,bkhd->bqhd', a1 - lam * a2, vf)

    qb = q.reshape(bt, nblk, blk, ATT_HEADS, ATT_HEAD_DIM).transpose(1, 0, 2, 3, 4)
    qpb = q_pos.reshape(nblk, blk)
    o = lax.map(block, (qb, qpb))
    o = o.transpose(1, 0, 2, 3, 4).reshape(bt, tq, ATT_HEADS, ATT_HEAD_DIM)
    o = o * lax.rsqrt(jnp.mean(o * o, axis=-1, keepdims=True) + EPS) * g_head.astype(f32)
    o = o * (1.0 - lam_init)
    return o.reshape(bt, tq, ATT_WIDTH).astype(q.dtype)


def causal_depthwise_conv(z, buf, w, b):
    zp = jnp.concatenate([buf.astype(z.dtype), z], axis=1)
    y = lax.conv_general_dilated(zp, w[:, None, :].astype(z.dtype), window_strides=(1,), padding='VALID',
                                 dimension_numbers=('NWC', 'WIO', 'NWC'), feature_group_count=z.shape[-1])
    return y + b.astype(z.dtype), zp[:, -(CONV_K - 1):]


def conformer_conv(p, buf, w, b, ln_g, ln_b, w_pw):
    z = p[..., :CONV_WIDTH] * jax.nn.sigmoid(p[..., CONV_WIDTH:])
    y, new_buf = causal_depthwise_conv(z, buf, w, b)
    y = jax.nn.silu(layer_norm(y, ln_g, ln_b))
    return y @ w_pw, new_buf


def chunk_spatial_gating(p, ln_g, ln_b, ws, bs):
    bt, t, _ = p.shape
    z = jax.nn.gelu(p)
    u, v = z[..., :GMLP_WIDTH], z[..., GMLP_WIDTH:]
    v = layer_norm(v, ln_g, ln_b)
    L = min(t, GMLP_CHUNK)
    nc = t // L
    mask = jnp.tril(jnp.ones((L, L), dtype=bool))
    w = jnp.where(mask, ws[:, :L, :L], 0.0).astype(v.dtype)
    vc = v.reshape(bt, nc, L, GMLP_HEADS, GMLP_HEAD_DIM)
    mixed = jnp.einsum('hij,bcjhd->bcihd', w, vc) + bs[:, :L].T.astype(v.dtype)[None, None, :, :, None]
    return u * mixed.reshape(bt, t, GMLP_WIDTH), v


def memory_cross_attention(h, mem_k, mem_v, wq, wo):
    f32 = jnp.float32
    bt, t, _ = h.shape
    q = (h @ wq).reshape(bt, t, X_HEADS, X_HEAD_DIM).astype(f32) * (X_HEAD_DIM ** -0.5)
    a = jax.nn.softmax(jnp.einsum('bqhd,bkhd->bhqk', q, mem_k.astype(f32)), axis=-1)
    o = jnp.einsum('bhqk,bkhd->bqhd', a, mem_v.astype(f32)).reshape(bt, t, X_WIDTH)
    return o.astype(h.dtype) @ wo


def encoder_layer(x, prm, lam_init, attn_past, s0_re, s0_im, conv_buf, mem_k, mem_v):
    f32 = jnp.float32
    bt, t, _ = x.shape
    x = x + 0.5 * swiglu_ffn(rms_norm(x, prm['ffn1_norm']), prm['ffn1_w_gate'], prm['ffn1_w_up'], prm['ffn1_w_down'])
    h = rms_norm(x, prm['mix_norm'])
    proj = h @ prm['w_in']
    a_out, s_re, s_im = s5_mixer(proj[..., OFF_SSM:OFF_Q], s0_re, s0_im, prm['ssm_a_re'], prm['ssm_a_im'],
                                 prm['ssm_b_re'], prm['ssm_b_im'], prm['ssm_c_re'], prm['ssm_c_im'],
                                 prm['ssm_d'], prm['ssm_log_dt'], prm['ssm_w_glu'], prm['ssm_b_glu'])
    q = proj[..., OFF_Q:OFF_K].reshape(bt, t, ATT_HEADS, ATT_HEAD_DIM)
    k = proj[..., OFF_K:OFF_V].reshape(bt, t, ATT_HEADS, ATT_HEAD_DIM)
    v = proj[..., OFF_V:OFF_CONV].reshape(bt, t, ATT_HEADS, ATT_HEAD_DIM)
    if attn_past is None:
        past = 0
        k_all, v_all = k, v
    else:
        past = attn_past[0].shape[1]
        k_all = jnp.concatenate([attn_past[0].astype(k.dtype), k], axis=1)
        v_all = jnp.concatenate([attn_past[1].astype(v.dtype), v], axis=1)
    q_pos = past + jnp.arange(t, dtype=jnp.int32)
    k_pos = jnp.arange(past + t, dtype=jnp.int32)
    lam = (jnp.exp(jnp.dot(prm['lq1'].astype(f32), prm['lk1'].astype(f32)))
           - jnp.exp(jnp.dot(prm['lq2'].astype(f32), prm['lk2'].astype(f32))) + lam_init)
    b_out = diff_attention(q, k_all, v_all, q_pos, k_pos, lam, lam_init, prm['dattn_norm'])
    c_out, new_buf = conformer_conv(proj[..., OFF_CONV:OFF_GMLP], conv_buf, prm['conv_w'], prm['conv_b'],
                                    prm['conv_ln_g'], prm['conv_ln_b'], prm['conv_w_pw'])
    d_out, gmlp_v = chunk_spatial_gating(proj[..., OFF_GMLP:IN_WIDTH], prm['gmlp_ln_g'], prm['gmlp_ln_b'],
                                         prm['gmlp_ws'], prm['gmlp_bs'])
    mix = jnp.concatenate([a_out, b_out, c_out.astype(x.dtype), d_out.astype(x.dtype)], axis=-1)
    x = x + mix @ prm['w_out']
    x = x + memory_cross_attention(rms_norm(x, prm['xattn_norm']), mem_k, mem_v, prm['xattn_wq'], prm['xattn_wo'])
    x = x + 0.5 * swiglu_ffn(rms_norm(x, prm['ffn2_norm']), prm['ffn2_w_gate'], prm['ffn2_w_up'], prm['ffn2_w_down'])
    return x, k, v, s_re, s_im, new_buf, gmlp_v


def setup_inputs(seed: int = 0) -> dict:
    key = jax.random.key(seed)
    ks = iter(jax.random.split(key, 80))
    f32 = jnp.float32
    L = DEPTH

    def nrm(shape, scale=1.0):
        return scale * jax.random.normal(next(ks), shape, f32)

    def gain(shape):
        return 1.0 + 0.05 * jax.random.normal(next(ks), shape, f32)

    n_idx = jnp.arange(SSM_STATE, dtype=f32)
    return {
        'x_prompt': nrm((BATCH, SEQ, D_MODEL)),
        'x_sample': nrm((DEC_BATCH, DEC_SEQ, D_MODEL)),
        'mem_prompt': nrm((BATCH, N_MEM, D_MODEL)),
        'cache_attn_k': nrm((L, DEC_BATCH, PAST_LEN, ATT_HEADS, ATT_HEAD_DIM)),
        'cache_attn_v': nrm((L, DEC_BATCH, PAST_LEN, ATT_HEADS, ATT_HEAD_DIM)),
        'state_ssm_re': nrm((L, DEC_BATCH, SSM_GROUPS, SSM_STATE), 0.5),
        'state_ssm_im': nrm((L, DEC_BATCH, SSM_GROUPS, SSM_STATE), 0.5),
        'state_conv': nrm((L, DEC_BATCH, CONV_K - 1, CONV_WIDTH), 0.5),
        'cache_mem_k': nrm((L, DEC_BATCH, N_MEM, X_HEADS, X_HEAD_DIM)),
        'cache_mem_v': nrm((L, DEC_BATCH, N_MEM, X_HEADS, X_HEAD_DIM)),
        'ffn1_norm': gain((L, D_MODEL)),
        'ffn1_w_gate': nrm((L, D_MODEL, D_FF), D_MODEL ** -0.5),
        'ffn1_w_up': nrm((L, D_MODEL, D_FF), D_MODEL ** -0.5),
        'ffn1_w_down': nrm((L, D_FF, D_MODEL), D_FF ** -0.5),
        'mix_norm': gain((L, D_MODEL)),
        'w_in': nrm((L, D_MODEL, IN_WIDTH), D_MODEL ** -0.5),
        'w_out': nrm((L, MIX_WIDTH, D_MODEL), MIX_WIDTH ** -0.5),
        'ssm_a_re': -0.5 + 0.01 * nrm((L, SSM_GROUPS, SSM_STATE)),
        'ssm_a_im': math.pi * n_idx + 0.01 * nrm((L, SSM_GROUPS, SSM_STATE)),
        'ssm_b_re': nrm((L, SSM_GROUPS, SSM_STATE, SSM_GROUP), (2 * SSM_GROUP) ** -0.5),
        'ssm_b_im': nrm((L, SSM_GROUPS, SSM_STATE, SSM_GROUP), (2 * SSM_GROUP) ** -0.5),
        'ssm_c_re': nrm((L, SSM_GROUPS, SSM_GROUP, SSM_STATE), (2 * SSM_STATE) ** -0.5),
        'ssm_c_im': nrm((L, SSM_GROUPS, SSM_GROUP, SSM_STATE), (2 * SSM_STATE) ** -0.5),
        'ssm_d': nrm((L, SSM_GROUPS, SSM_GROUP)),
        'ssm_log_dt': jax.random.uniform(next(ks), (L, SSM_GROUPS), f32, math.log(SSM_DT_MIN), math.log(SSM_DT_MAX)),
        'ssm_w_glu': nrm((L, SSM_WIDTH, SSM_WIDTH), SSM_WIDTH ** -0.5),
        'ssm_b_glu': nrm((L, SSM_WIDTH), 0.02),
        'dattn_lq1': nrm((L, ATT_HALF), 0.1),
        'dattn_lk1': nrm((L, ATT_HALF), 0.1),
        'dattn_lq2': nrm((L, ATT_HALF), 0.1),
        'dattn_lk2': nrm((L, ATT_HALF), 0.1),
        'dattn_norm': gain((L, ATT_HEAD_DIM)),
        'conv_w': nrm((L, CONV_K, CONV_WIDTH), CONV_K ** -0.5),
        'conv_b': nrm((L, CONV_WIDTH), 0.02),
        'conv_ln_g': gain((L, CONV_WIDTH)),
        'conv_ln_b': nrm((L, CONV_WIDTH), 0.02),
        'conv_w_pw': nrm((L, CONV_WIDTH, CONV_WIDTH), CONV_WIDTH ** -0.5),
        'gmlp_ln_g': gain((L, GMLP_WIDTH)),
        'gmlp_ln_b': nrm((L, GMLP_WIDTH), 0.02),
        'gmlp_ws': nrm((L, GMLP_HEADS, GMLP_CHUNK, GMLP_CHUNK), GMLP_CHUNK ** -0.5),
        'gmlp_bs': gain((L, GMLP_HEADS, GMLP_CHUNK)),
        'xattn_norm': gain((L, D_MODEL)),
        'mem_norm': gain((L, D_MODEL)),
        'xattn_wq': nrm((L, D_MODEL, X_WIDTH), D_MODEL ** -0.5),
        'xattn_wk': nrm((L, D_MODEL, X_WIDTH), D_MODEL ** -0.5),
        'xattn_wv': nrm((L, D_MODEL, X_WIDTH), D_MODEL ** -0.5),
        'xattn_wo': nrm((L, X_WIDTH, D_MODEL), X_WIDTH ** -0.5),
        'ffn2_norm': gain((L, D_MODEL)),
        'ffn2_w_gate': nrm((L, D_MODEL, D_FF), D_MODEL ** -0.5),
        'ffn2_w_up': nrm((L, D_MODEL, D_FF), D_MODEL ** -0.5),
        'ffn2_w_down': nrm((L, D_FF, D_MODEL), D_FF ** -0.5),
        'final_norm': gain((D_MODEL,)),
    }


def reference(x_prompt, x_sample, mem_prompt, cache_attn_k, cache_attn_v, state_ssm_re, state_ssm_im,
              state_conv, cache_mem_k, cache_mem_v,
              ffn1_norm, ffn1_w_gate, ffn1_w_up, ffn1_w_down, mix_norm, w_in, w_out,
              ssm_a_re, ssm_a_im, ssm_b_re, ssm_b_im, ssm_c_re, ssm_c_im, ssm_d, ssm_log_dt, ssm_w_glu, ssm_b_glu,
              dattn_lq1, dattn_lk1, dattn_lq2, dattn_lk2, dattn_norm,
              conv_w, conv_b, conv_ln_g, conv_ln_b, conv_w_pw,
              gmlp_ln_g, gmlp_ln_b, gmlp_ws, gmlp_bs,
              xattn_norm, mem_norm, xattn_wq, xattn_wk, xattn_wv, xattn_wo,
              ffn2_norm, ffn2_w_gate, ffn2_w_up, ffn2_w_down, final_norm):
    xp, xs = x_prompt, x_sample
    bp = x_prompt.shape[0]
    p_k, p_v, p_sre, p_sim, p_conv, p_mk, p_mv = [], [], [], [], [], [], []
    s_k, s_v, s_sre, s_sim, s_conv, s_gv = [], [], [], [], [], []
    for l in range(DEPTH):
        prm = dict(ffn1_norm=ffn1_norm[l], ffn1_w_gate=ffn1_w_gate[l], ffn1_w_up=ffn1_w_up[l], ffn1_w_down=ffn1_w_down[l],
                   mix_norm=mix_norm[l], w_in=w_in[l], w_out=w_out[l],
                   ssm_a_re=ssm_a_re[l], ssm_a_im=ssm_a_im[l], ssm_b_re=ssm_b_re[l], ssm_b_im=ssm_b_im[l],
                   ssm_c_re=ssm_c_re[l], ssm_c_im=ssm_c_im[l], ssm_d=ssm_d[l], ssm_log_dt=ssm_log_dt[l],
                   ssm_w_glu=ssm_w_glu[l], ssm_b_glu=ssm_b_glu[l],
                   lq1=dattn_lq1[l], lk1=dattn_lk1[l], lq2=dattn_lq2[l], lk2=dattn_lk2[l], dattn_norm=dattn_norm[l],
                   conv_w=conv_w[l], conv_b=conv_b[l], conv_ln_g=conv_ln_g[l], conv_ln_b=conv_ln_b[l], conv_w_pw=conv_w_pw[l],
                   gmlp_ln_g=gmlp_ln_g[l], gmlp_ln_b=gmlp_ln_b[l], gmlp_ws=gmlp_ws[l], gmlp_bs=gmlp_bs[l],
                   xattn_norm=xattn_norm[l], xattn_wq=xattn_wq[l], xattn_wo=xattn_wo[l],
                   ffn2_norm=ffn2_norm[l], ffn2_w_gate=ffn2_w_gate[l], ffn2_w_up=ffn2_w_up[l], ffn2_w_down=ffn2_w_down[l])
        lam_init = 0.8 - 0.6 * math.exp(-0.3 * l)
        mem_h = rms_norm(mem_prompt, mem_norm[l])
        mk = (mem_h @ xattn_wk[l]).reshape(bp, N_MEM, X_HEADS, X_HEAD_DIM)
        mv = (mem_h @ xattn_wv[l]).reshape(bp, N_MEM, X_HEADS, X_HEAD_DIM)
        zs = jnp.zeros((bp, SSM_GROUPS, SSM_STATE), xp.dtype)
        zb = jnp.zeros((bp, CONV_K - 1, CONV_WIDTH), xp.dtype)
        xp, kp_, vp_, srp, sip, cbp, _ = encoder_layer(xp, prm, lam_init, None, zs, zs, zb, mk, mv)
        p_k.append(kp_); p_v.append(vp_); p_sre.append(srp); p_sim.append(sip)
        p_conv.append(cbp); p_mk.append(mk); p_mv.append(mv)
        xs, ks_, vs_, srs, sis, cbs, gvs = encoder_layer(
            xs, prm, lam_init, (cache_attn_k[l], cache_attn_v[l]), state_ssm_re[l], state_ssm_im[l],
            state_conv[l], cache_mem_k[l], cache_mem_v[l])
        s_k.append(ks_); s_v.append(vs_); s_sre.append(srs); s_sim.append(sis)
        s_conv.append(cbs); s_gv.append(gvs)
    y_prompt = rms_norm(xp, final_norm)
    y_sample = rms_norm(xs, final_norm)
    return (y_prompt, y_sample,
            jnp.stack(p_k), jnp.stack(p_v), jnp.stack(p_sre), jnp.stack(p_sim), jnp.stack(p_conv),
            jnp.stack(p_mk), jnp.stack(p_mv),
            jnp.stack(s_k), jnp.stack(s_v), jnp.stack(s_sre), jnp.stack(s_sim), jnp.stack(s_conv),
            jnp.stack(s_gv))
```

```python
import functools
import math

import jax
import jax.numpy as jnp
from jax import lax
from jax.experimental import pallas as pl
from jax.experimental.pallas import tpu as pltpu

F32 = jnp.float32
BF16 = jnp.bfloat16
EPS = 1e-6

D_MODEL = 1024
D_FF = 4096
DEPTH = 4
CHUNK_SHIFT = 6
N_MEM = 256
SSM_WIDTH = 256
SSM_GROUP = 16
SSM_GROUPS = 16
SSM_STATE = 64
SSM_COLS = SSM_GROUPS * SSM_STATE
SSM_TILES = SSM_COLS // 128
ATT_HEADS = 4
ATT_HEAD_DIM = 64
ATT_HALF = 32
ATT_GROUPS = 2 * ATT_HEADS
CONV_K = 31
GMLP_HEADS = 4
GMLP_CHUNK = 128
X_HEADS = 4
X_HEAD_DIM = 128
X_WIDTH = 512
LANES = 128
SUBLANES = 8
VMEM_LIMIT = 56 * 1024 * 1024


def _cparams(*sem):
    return pltpu.CompilerParams(dimension_semantics=sem, vmem_limit_bytes=VMEM_LIMIT)


def _rms(x, g):
    return x * lax.rsqrt(jnp.mean(x * x, axis=-1, keepdims=True) + EPS) * g


def _layer_norm(x, g, b):
    mu = jnp.mean(x, axis=-1, keepdims=True)
    xc = x - mu
    var = jnp.mean(xc * xc, axis=-1, keepdims=True)
    return xc * lax.rsqrt(var + EPS) * g + b


def _gelu(x):
    return 0.5 * x * (1.0 + jnp.tanh(math.sqrt(2.0 / math.pi) * (x + 0.044715 * (x * x * x))))


def _sigmoid(x):
    return 1.0 / (1.0 + jnp.exp(-x))


def _dot(a, b):
    return jnp.dot(a, b, preferred_element_type=F32)


def _dot_nt(a, b):
    return lax.dot_general(a, b, (((1,), (1,)), ((), ())), preferred_element_type=F32)


def _ffn_body(x_ref, g_ref, wg_ref, wu_ref, wd_ref, fg_ref, o_ref, h_ref, acc_ref, *, nj, final):
    j = pl.program_id(1)

    @pl.when(j == 0)
    def _():
        h_ref[...] = _rms(x_ref[...], g_ref[...]).astype(BF16)
        acc_ref[...] = jnp.zeros_like(acc_ref)

    h = h_ref[...]
    a = _dot(h, wg_ref[...])
    b = _dot(h, wu_ref[...])
    act = (a * _sigmoid(a) * b).astype(BF16)
    acc_ref[...] += _dot(act, wd_ref[...])

    @pl.when(j == nj - 1)
    def _():
        y = x_ref[...] + 0.5 * acc_ref[...]
        if final:
            y = _rms(y, fg_ref[...])
        o_ref[...] = y


def _ffn(x, l, norm, wg, wu, wd, final_g, *, final, tm, tf):
    n = x.shape[0]
    nj = D_FF // tf
    return pl.pallas_call(
        functools.partial(_ffn_body, nj=nj, final=final),
        grid=(n // tm, nj),
        in_specs=[
            pl.BlockSpec((tm, D_MODEL), lambda i, j: (i, 0)),
            pl.BlockSpec((None, 1, D_MODEL), lambda i, j: (l, 0, 0)),
            pl.BlockSpec((None, D_MODEL, tf), lambda i, j: (l, 0, j)),
            pl.BlockSpec((None, D_MODEL, tf), lambda i, j: (l, 0, j)),
            pl.BlockSpec((None, tf, D_MODEL), lambda i, j: (l, j, 0)),
            pl.BlockSpec((1, D_MODEL), lambda i, j: (0, 0)),
        ],
        out_specs=pl.BlockSpec((tm, D_MODEL), lambda i, j: (i, 0)),
        out_shape=jax.ShapeDtypeStruct((n, D_MODEL), F32),
        scratch_shapes=[pltpu.VMEM((tm, D_MODEL), BF16), pltpu.VMEM((tm, D_MODEL), F32)],
        name="ffn",
        compiler_params=_cparams("parallel", "arbitrary"),
    )(x, norm, wg, wu, wd, final_g)


PROJ_SPLITS = (("u", 0, 256), ("q", 256, 256), ("k", 512, 256), ("v", 768, 256),
               ("c", 1024, 512), ("g", 1536, 512))


def _proj_body(x_ref, g_ref, w_ref, *o_refs):
    h = _rms(x_ref[...], g_ref[...]).astype(BF16)
    for (_, off, width), o_ref in zip(PROJ_SPLITS, o_refs):
        o_ref[...] = _dot(h, w_ref[:, off:off + width])


def _proj_in(x, l, norm, w_in, *, tm):
    n = x.shape[0]
    return pl.pallas_call(
        _proj_body,
        grid=(n // tm,),
        in_specs=[
            pl.BlockSpec((tm, D_MODEL), lambda i: (i, 0)),
            pl.BlockSpec((None, 1, D_MODEL), lambda i: (l, 0, 0)),
            pl.BlockSpec((None, D_MODEL, 2048), lambda i: (l, 0, 0)),
        ],
        out_specs=[pl.BlockSpec((tm, w), lambda i: (i, 0)) for _, _, w in PROJ_SPLITS],
        out_shape=[jax.ShapeDtypeStruct((n, w), F32) for _, _, w in PROJ_SPLITS],
        name="proj_in",
        compiler_params=_cparams("parallel"),
    )(x, norm, w_in)


def _cmul(ar, ai, br, bi):
    return ar * br - ai * bi, ar * bi + ai * br


def _s5_body(u_ref, s0_ref, ab_ref, bblk_ref, cblk_ref, d_ref, wglu_ref, bglu_ref,
             y_ref, sout_ref, w_ref, carry_ref, e_ref, sin_ref, *, lc, nt, chained, tpc):
    t = pl.program_id(1)
    u = u_ref[...]
    bu = _dot(u.astype(BF16), bblk_ref[...])
    for c in range(2 * SSM_TILES):
        w_ref[c] = bu[:, c * LANES:(c + 1) * LANES]

    if chained:
        @pl.when(t == 0)
        def _():
            carry_ref[...] = s0_ref[...]

    def rows(j):
        return pl.ds(j, SUBLANES, stride=lc)

    def cols(c):
        return slice(c * LANES, (c + 1) * LANES)

    for c0 in range(0, SSM_TILES, tpc):
        re_t = list(range(c0, c0 + tpc))
        im_t = [SSM_TILES + c for c in re_t]
        ar = [jnp.broadcast_to(ab_ref[0:1, cols(c)], (SUBLANES, LANES)) for c in re_t]
        ai = [jnp.broadcast_to(ab_ref[1:2, cols(c)], (SUBLANES, LANES)) for c in re_t]
        if chained:
            zr = [jnp.zeros((SUBLANES, LANES), F32)] * tpc
            zi = [jnp.zeros((SUBLANES, LANES), F32)] * tpc
        else:
            zr = [s0_ref[:, cols(c)] for c in re_t]
            zi = [s0_ref[:, cols(c)] for c in im_t]
        for j in range(lc):
            for n in range(tpc):
                pr, pi = _cmul(ar[n], ai[n], zr[n], zi[n])
                zr[n] = pr + w_ref[re_t[n], rows(j), :]
                zi[n] = pi + w_ref[im_t[n], rows(j), :]
                w_ref[re_t[n], rows(j), :] = zr[n]
                w_ref[im_t[n], rows(j), :] = zi[n]
        if not chained:
            for n in range(tpc):
                sout_ref[:, cols(re_t[n])] = zr[n]
                sout_ref[:, cols(im_t[n])] = zi[n]
            continue
        for n in range(tpc):
            e_ref[:, cols(re_t[n])] = zr[n]
            e_ref[:, cols(im_t[n])] = zi[n]
        lo, hi = c0 * LANES, (c0 + tpc) * LANES
        re_c, im_c = slice(lo, hi), slice(SSM_COLS + lo, SSM_COLS + hi)
        pr, pi = ab_ref[0:1, re_c], ab_ref[1:2, re_c]
        for _ in range(int(math.log2(lc))):
            pr, pi = _cmul(pr, pi, pr, pi)
        fr = carry_ref[0:1, re_c]
        fi = carry_ref[0:1, im_c]
        for s in range(SUBLANES):
            sin_ref[s:s + 1, re_c] = fr
            sin_ref[s:s + 1, im_c] = fi
            gr, gi = _cmul(pr, pi, fr, fi)
            fr = gr + e_ref[s:s + 1, re_c]
            fi = gi + e_ref[s:s + 1, im_c]
        carry_ref[0:1, re_c] = fr
        carry_ref[0:1, im_c] = fi
        cr = [sin_ref[:, cols(c)] for c in re_t]
        ci = [sin_ref[:, cols(c)] for c in im_t]
        for j in range(lc):
            for n in range(tpc):
                cr[n], ci[n] = _cmul(ar[n], ai[n], cr[n], ci[n])
                w_ref[re_t[n], rows(j), :] = w_ref[re_t[n], rows(j), :] + cr[n]
                w_ref[im_t[n], rows(j), :] = w_ref[im_t[n], rows(j), :] + ci[n]

    if chained:
        @pl.when(t == nt - 1)
        def _():
            sout_ref[...] = carry_ref[...]

    states = jnp.concatenate([w_ref[c].astype(BF16) for c in range(2 * SSM_TILES)], axis=1)
    y = _dot(states, cblk_ref[...]) + d_ref[...] * u
    g = _gelu(y)
    y_ref[...] = g * _sigmoid(_dot(g.astype(BF16), wglu_ref[...]) + bglu_ref[...])


def _s5(u_all, row0, s0, prm, *, nb, t_len, chained):
    if chained:
        tt = min(t_len, 256)
        nt = t_len // tt
        grid = (nb, nt)
        lc = tt // SUBLANES
        blk0 = row0 // tt
        u_map = lambda b, t: (blk0 + b * nt + t, 0)
        y_map = lambda b, t: (b * nt + t, 0)
        s_spec = pl.BlockSpec((None, 1, 2 * SSM_COLS), lambda b, t: (b, 0, 0))
        s0 = s0.reshape(nb, 1, 2 * SSM_COLS)
        s_shape = jax.ShapeDtypeStruct((nb, 1, 2 * SSM_COLS), F32)
    else:
        tt = SUBLANES * t_len
        nt = 1
        grid = (nb // SUBLANES, 1)
        lc = t_len
        blk0 = row0 // tt
        u_map = lambda b, t: (blk0 + b, 0)
        y_map = lambda b, t: (b, 0)
        s_spec = pl.BlockSpec((SUBLANES, 2 * SSM_COLS), lambda b, t: (b, 0))
        s_shape = jax.ShapeDtypeStruct((nb, 2 * SSM_COLS), F32)
    full = lambda shape: pl.BlockSpec(shape, lambda b, t: (0,) * len(shape))
    y, s_out = pl.pallas_call(
        functools.partial(_s5_body, lc=lc, nt=nt, chained=chained, tpc=4),
        grid=grid,
        in_specs=[
            pl.BlockSpec((tt, SSM_WIDTH), u_map),
            s_spec,
            full((2, SSM_COLS)),
            full((SSM_WIDTH, 2 * SSM_COLS)),
            full((2 * SSM_COLS, SSM_WIDTH)),
            full((1, SSM_WIDTH)),
            full((SSM_WIDTH, SSM_WIDTH)),
            full((1, SSM_WIDTH)),
        ],
        out_specs=[pl.BlockSpec((tt, SSM_WIDTH), y_map), s_spec],
        out_shape=[jax.ShapeDtypeStruct((nb * t_len, SSM_WIDTH), F32), s_shape],
        scratch_shapes=[
            pltpu.VMEM((2 * SSM_TILES, tt, LANES), F32),
            pltpu.VMEM((1, 2 * SSM_COLS), F32),
            pltpu.VMEM((SUBLANES, 2 * SSM_COLS), F32),
            pltpu.VMEM((SUBLANES, 2 * SSM_COLS), F32),
        ],
        name="s5_chained" if chained else "s5_batched",
        compiler_params=_cparams("parallel", "arbitrary"),
    )(u_all, s0, prm["ab"], prm["bblk"], prm["cblk"], prm["d"], prm["wglu"], prm["bglu"])
    return y, s_out.reshape(nb, 2 * SSM_COLS)


def _s5_params(a_re, a_im, b_re, b_im, c_re, c_im, d, log_dt, w_glu, b_glu):
    lam = lax.complex(a_re, a_im)
    dt = jnp.exp(log_dt)[:, None]
    a_bar = jnp.exp(lam * dt)
    b_bar = ((a_bar - 1.0) / lam)[..., None] * lax.complex(b_re, b_im)
    eye = jnp.eye(SSM_GROUPS, dtype=F32)

    def blk_in(m):
        return jnp.einsum("gpc,gh->gchp", m, eye).reshape(SSM_WIDTH, SSM_COLS)

    def blk_out(m):
        return jnp.einsum("gcp,gh->gphc", m, eye).reshape(SSM_COLS, SSM_WIDTH)

    return dict(
        ab=jnp.stack([jnp.real(a_bar).reshape(-1), jnp.imag(a_bar).reshape(-1)]).astype(F32),
        bblk=jnp.concatenate([blk_in(jnp.real(b_bar)), blk_in(jnp.imag(b_bar))], axis=1).astype(BF16),
        cblk=jnp.concatenate([blk_out(c_re), -blk_out(c_im)], axis=0).astype(BF16),
        d=d.reshape(1, SSM_WIDTH),
        wglu=w_glu.astype(BF16),
        bglu=b_glu.reshape(1, SSM_WIDTH),
    )


def _alibi_slope(h):
    return 2.0 ** (-8.0 * (h + 1) / ATT_HEADS)


def _stack_queries(q, qs_ref):
    lane = lax.broadcasted_iota(jnp.int32, q.shape, 1)
    qsc = q * (ATT_HALF ** -0.5)
    for g in range(ATT_GROUPS):
        lo = g * ATT_HALF
        qs_ref[g] = jnp.where((lane >= lo) & (lane < lo + ATT_HALF), qsc, 0.0).astype(BF16)


def _attn_finish(acc, linv, lq_ref, lk_ref, gh_ref, lam_init):
    lam = (jnp.exp(jnp.sum(lq_ref[0:1, :] * lk_ref[0:1, :], axis=-1, keepdims=True))
           - jnp.exp(jnp.sum(lq_ref[1:2, :] * lk_ref[1:2, :], axis=-1, keepdims=True)) + lam_init)
    outs = []
    for pair in range(ATT_HEADS // 2):
        os_ = []
        for h in (2 * pair, 2 * pair + 1):
            os_.append(acc(2 * h) * linv(2 * h) - lam * (acc(2 * h + 1) * linv(2 * h + 1)))
        lane = lax.broadcasted_iota(jnp.int32, os_[0].shape, 1)
        first = lane < ATT_HEAD_DIM
        o = jnp.where(first, os_[0], os_[1])
        sq = o * o
        ss_all = jnp.sum(sq, axis=-1, keepdims=True)
        ss_a = jnp.sum(jnp.where(first, sq, 0.0), axis=-1, keepdims=True)
        ms = jnp.where(first, ss_a, ss_all - ss_a) * (1.0 / ATT_HEAD_DIM)
        outs.append(o * lax.rsqrt(ms + EPS) * gh_ref[...] * (1.0 - lam_init))
    return jnp.concatenate(outs, axis=-1)


def _attn_prompt_body(q_ref, k_ref, v_ref, lq_ref, lk_ref, gh_ref, o_ref,
                      qs_ref, m_ref, l_ref, acc_ref, *, tq, lam_init):
    i = pl.program_id(1)
    j = pl.program_id(2)

    @pl.when(j == 0)
    def _():
        _stack_queries(q_ref[...], qs_ref)
        m_ref[...] = jnp.full_like(m_ref, -jnp.inf)
        l_ref[...] = jnp.zeros_like(l_ref)
        acc_ref[...] = jnp.zeros_like(acc_ref)

    @pl.when(j <= i)
    def _():
        kb = k_ref[...].astype(BF16)
        vb = v_ref[...].astype(BF16)
        qpos = i * tq + lax.broadcasted_iota(jnp.int32, (tq, tq), 0)
        kpos = j * tq + lax.broadcasted_iota(jnp.int32, (tq, tq), 1)
        visible = jnp.right_shift(kpos, CHUNK_SHIFT) <= jnp.right_shift(qpos, CHUNK_SHIFT)
        ndist = -jnp.abs(qpos - kpos).astype(F32)
        for g in range(ATT_GROUPS):
            h = g // 2
            s = _dot_nt(qs_ref[g], kb)
            s = jnp.where(visible, s + _alibi_slope(h) * ndist, -jnp.inf)
            m_prev = m_ref[g]
            m_new = jnp.maximum(m_prev, jnp.max(s, axis=-1, keepdims=True))
            alpha = jnp.exp(m_prev - m_new)
            p = jnp.exp(s - m_new[:, 0:1])
            l_ref[g] = alpha * l_ref[g] + jnp.sum(p, axis=-1, keepdims=True)
            vcol = (h // 2) * LANES
            acc_ref[g] = alpha * acc_ref[g] + _dot(p.astype(BF16), vb[:, vcol:vcol + LANES])
            m_ref[g] = m_new

    @pl.when(j == i)
    def _():
        o_ref[...] = _attn_finish(lambda g: acc_ref[g], lambda g: 1.0 / l_ref[g][:, 0:1],
                                  lq_ref, lk_ref, gh_ref, lam_init)


def _attn_prompt(q_all, k_all, v_all, l, lq, lk, gh, lam_init, *, nb, t_len, tq):
    nq = t_len // tq
    q_map = lambda b, i, j: (b * nq + i, 0)
    k_map = lambda b, i, j: (b * nq + jnp.minimum(i, j), 0)
    par = lambda shape: pl.BlockSpec((None,) + shape, lambda b, i, j: (l, 0, 0))
    return pl.pallas_call(
        functools.partial(_attn_prompt_body, tq=tq, lam_init=lam_init),
        grid=(nb, nq, nq),
        in_specs=[
            pl.BlockSpec((tq, 256), q_map),
            pl.BlockSpec((tq, 256), k_map),
            pl.BlockSpec((tq, 256), k_map),
            par((2, ATT_HALF)), par((2, ATT_HALF)), par((1, LANES)),
        ],
        out_specs=pl.BlockSpec((tq, 256), q_map),
        out_shape=jax.ShapeDtypeStruct((nb * t_len, 256), F32),
        scratch_shapes=[
            pltpu.VMEM((ATT_GROUPS, tq, 256), BF16),
            pltpu.VMEM((ATT_GROUPS, tq, LANES), F32),
            pltpu.VMEM((ATT_GROUPS, tq, LANES), F32),
            pltpu.VMEM((ATT_GROUPS, tq, LANES), F32),
        ],
        name="attn_prompt",
        compiler_params=_cparams("parallel", "parallel", "arbitrary"),
    )(q_all, k_all, v_all, lq, lk, gh)


def _attn_sample_body(q_ref, kn_ref, vn_ref, kc_ref, vc_ref, lq_ref, lk_ref, gh_ref, o_ref,
                      qs_ref, acc_ref, linv_ref, *, tq, past, lam_init):
    _stack_queries(q_ref[...], qs_ref)
    kc = kc_ref[...].astype(BF16)
    vc = vc_ref[...].astype(BF16)
    kn = kn_ref[...].astype(BF16)
    vn = vn_ref[...].astype(BF16)
    r_c = lax.broadcasted_iota(jnp.int32, (tq, past), 0)
    c_c = lax.broadcasted_iota(jnp.int32, (tq, past), 1)
    nd_c = -(past + r_c - c_c).astype(F32)
    r_n = lax.broadcasted_iota(jnp.int32, (tq, tq), 0)
    c_n = lax.broadcasted_iota(jnp.int32, (tq, tq), 1)
    nd_n = -jnp.abs(r_n - c_n).astype(F32)
    for g in range(ATT_GROUPS):
        h = g // 2
        s_c = _dot_nt(qs_ref[g], kc) + _alibi_slope(h) * nd_c
        s_n = _dot_nt(qs_ref[g], kn) + _alibi_slope(h) * nd_n
        m = jnp.maximum(jnp.max(s_c, axis=-1, keepdims=True), jnp.max(s_n, axis=-1, keepdims=True))
        p_c = jnp.exp(s_c - m)
        p_n = jnp.exp(s_n - m)
        lsum = jnp.sum(p_c, axis=-1, keepdims=True) + jnp.sum(p_n, axis=-1, keepdims=True)
        vcol = (h // 2) * LANES
        acc_ref[g] = (_dot(p_c.astype(BF16), vc[:, vcol:vcol + LANES])
                      + _dot(p_n.astype(BF16), vn[:, vcol:vcol + LANES]))
        linv_ref[g] = jnp.broadcast_to(1.0 / lsum, (tq, LANES))
    o_ref[...] = _attn_finish(lambda g: acc_ref[g], lambda g: linv_ref[g][:, 0:1],
                              lq_ref, lk_ref, gh_ref, lam_init)


def _attn_sample(q_all, k_all, v_all, row0, cache_k, cache_v, l, lq, lk, gh, lam_init, *, nb, t_len):
    past = cache_k.shape[2]
    blk0 = row0 // t_len
    new_map = lambda b: (blk0 + b, 0)
    par = lambda shape: pl.BlockSpec((None,) + shape, lambda b: (l, 0, 0))
    cache_spec = pl.BlockSpec((None, None, past, 256), lambda b: (l, b, 0, 0))
    return pl.pallas_call(
        functools.partial(_attn_sample_body, tq=t_len, past=past, lam_init=lam_init),
        grid=(nb,),
        in_specs=[
            pl.BlockSpec((t_len, 256), new_map),
            pl.BlockSpec((t_len, 256), new_map),
            pl.BlockSpec((t_len, 256), new_map),
            cache_spec, cache_spec,
            par((2, ATT_HALF)), par((2, ATT_HALF)), par((1, LANES)),
        ],
        out_specs=pl.BlockSpec((t_len, 256), lambda b: (b, 0)),
        out_shape=jax.ShapeDtypeStruct((nb * t_len, 256), F32),
        scratch_shapes=[
            pltpu.VMEM((ATT_GROUPS, t_len, 256), BF16),
            pltpu.VMEM((ATT_GROUPS, t_len, LANES), F32),
            pltpu.VMEM((ATT_GROUPS, t_len, LANES), F32),
        ],
        name="attn_sample",
        compiler_params=_cparams("parallel"),
    )(q_all, k_all, v_all, cache_k, cache_v, lq, lk, gh)


CONV_PAD = 32
CONV_ROWS = 64


def _conv_body(p_ref, buf_ref, w_ref, b_ref, lg_ref, lb_ref, wpw_ref, o_ref, nbuf_ref, zp_ref, *, tt, nt):
    t = pl.program_id(1)
    off = CONV_PAD - (CONV_K - 1)

    @pl.when(t == 0)
    def _():
        zp_ref[0:off, :] = jnp.zeros((off, 256), F32)
        zp_ref[off:CONV_PAD, :] = buf_ref[...]

    @pl.when(t > 0)
    def _():
        zp_ref[0:CONV_PAD, :] = zp_ref[tt:tt + CONV_PAD, :]

    p = p_ref[...]
    zp_ref[CONV_PAD:CONV_PAD + tt, :] = p[:, 0:256] * _sigmoid(p[:, 256:512])
    for r0 in range(0, tt, CONV_ROWS):
        acc = jnp.broadcast_to(b_ref[...], (CONV_ROWS, 256))
        for k in range(CONV_K):
            acc = acc + zp_ref[r0 + off + k:r0 + off + k + CONV_ROWS, :] * w_ref[k:k + 1, :]
        y = _layer_norm(acc, lg_ref[...], lb_ref[...])
        y = y * _sigmoid(y)
        o_ref[r0:r0 + CONV_ROWS, :] = _dot(y.astype(BF16), wpw_ref[...])

    @pl.when(t == nt - 1)
    def _():
        nbuf_ref[...] = zp_ref[tt + off:tt + CONV_PAD, :]


def _conv(p_all, row0, buf0, l, conv_w, conv_b, ln_g, ln_b, w_pw, *, nb, t_len):
    tt = min(t_len, 512)
    nt = t_len // tt
    blk0 = row0 // tt
    par = lambda shape: pl.BlockSpec((None,) + shape, lambda b, t: (l,) + (0,) * len(shape))
    buf_spec = pl.BlockSpec((None, CONV_K - 1, 256), lambda b, t: (b, 0, 0))
    return pl.pallas_call(
        functools.partial(_conv_body, tt=tt, nt=nt),
        grid=(nb, nt),
        in_specs=[
            pl.BlockSpec((tt, 512), lambda b, t: (blk0 + b * nt + t, 0)),
            buf_spec,
            par((CONV_K, 256)), par((1, 256)), par((1, 256)), par((1, 256)), par((256, 256)),
        ],
        out_specs=[pl.BlockSpec((tt, 256), lambda b, t: (b * nt + t, 0)), buf_spec],
        out_shape=[jax.ShapeDtypeStruct((nb * t_len, 256), F32),
                   jax.ShapeDtypeStruct((nb, CONV_K - 1, 256), F32)],
        scratch_shapes=[pltpu.VMEM((CONV_PAD + tt, 256), F32)],
        name="conformer_conv",
        compiler_params=_cparams("parallel", "arbitrary"),
    )(p_all, buf0, conv_w, conv_b, ln_g, ln_b, w_pw)


def _gmlp_body(p_ref, lg_ref, lb_ref, ws_ref, bs_ref, o_ref, v_ref, *, lch):
    z = _gelu(p_ref[...])
    u = z[:, 0:256]
    v = _layer_norm(z[:, 256:512], lg_ref[...], lb_ref[...])
    v_ref[...] = v
    vb = v.astype(BF16)
    row = lax.broadcasted_iota(jnp.int32, (lch, lch), 0)
    col = lax.broadcasted_iota(jnp.int32, (lch, lch), 1)
    lane = lax.broadcasted_iota(jnp.int32, (lch, 256), 1)
    mixed = None
    for h in range(GMLP_HEADS):
        w = jnp.where(row >= col, ws_ref[h, 0:lch, 0:lch], 0.0).astype(BF16)
        mh = _dot(w, vb)
        mixed = mh if mixed is None else jnp.where(lane >= h * 64, mh, mixed)
    o_ref[...] = u * (mixed + bs_ref[0:lch, :])


def _gmlp(p_all, row0, l, ln_g, ln_b, ws, bs_wide, *, nb, t_len):
    lch = min(t_len, GMLP_CHUNK)
    nc = t_len // lch
    blk0 = row0 // lch
    par = lambda shape: pl.BlockSpec((None,) + shape, lambda b: (l,) + (0,) * len(shape))
    return pl.pallas_call(
        functools.partial(_gmlp_body, lch=lch),
        grid=(nb * nc,),
        in_specs=[
            pl.BlockSpec((lch, 512), lambda b: (blk0 + b, 0)),
            par((1, 256)), par((1, 256)), par((GMLP_HEADS, GMLP_CHUNK, GMLP_CHUNK)),
            par((GMLP_CHUNK, 256)),
        ],
        out_specs=[pl.BlockSpec((lch, 256), lambda b: (b, 0)), pl.BlockSpec((lch, 256), lambda b: (b, 0))],
        out_shape=[jax.ShapeDtypeStruct((nb * t_len, 256), F32)] * 2,
        name="gmlp",
        compiler_params=_cparams("parallel"),
    )(p_all, ln_g, ln_b, ws, bs_wide)


def _memkv_body(m_ref, g_ref, wk_ref, wv_ref, k_ref, v_ref):
    h = _rms(m_ref[...], g_ref[...]).astype(BF16)
    k_ref[...] = _dot(h, wk_ref[...])
    v_ref[...] = _dot(h, wv_ref[...])


def _memkv(mem, l, norm, wk, wv):
    nb = mem.shape[0]
    par = lambda shape: pl.BlockSpec((None,) + shape, lambda b: (l,) + (0,) * len(shape))
    io = pl.BlockSpec((None, N_MEM, X_WIDTH), lambda b: (b, 0, 0))
    return pl.pallas_call(
        _memkv_body,
        grid=(nb,),
        in_specs=[pl.BlockSpec((None, N_MEM, D_MODEL), lambda b: (b, 0, 0)),
                  par((1, D_MODEL)), par((D_MODEL, X_WIDTH)), par((D_MODEL, X_WIDTH))],
        out_specs=[io, io],
        out_shape=[jax.ShapeDtypeStruct((nb, N_MEM, X_WIDTH), F32)] * 2,
        name="mem_kv",
        compiler_params=_cparams("parallel"),
    )(mem, norm, wk, wv)


def _mix_xattn_body(x_ref, a_ref, b_ref, c_ref, d_ref, wout_ref, g_ref, wq_ref, wo_ref, mk_ref, mv_ref,
                    o_ref, oh_ref, *, nbs, tpb):
    x = x_ref[...]
    for idx, m_ref in enumerate((a_ref, b_ref, c_ref, d_ref)):
        x = x + _dot(m_ref[...].astype(BF16), wout_ref[idx * 256:(idx + 1) * 256, :])
    h = _rms(x, g_ref[...]).astype(BF16)
    q = (_dot(h, wq_ref[...]) * (X_HEAD_DIM ** -0.5)).astype(BF16)
    for bi in range(nbs):
        r0 = bi * tpb
        for hd in range(X_HEADS):
            c0 = hd * X_HEAD_DIM
            kh = mk_ref[bi, :, c0:c0 + X_HEAD_DIM].astype(BF16)
            vh = mv_ref[bi, :, c0:c0 + X_HEAD_DIM].astype(BF16)
            s = _dot_nt(q[r0:r0 + tpb, c0:c0 + X_HEAD_DIM], kh)
            p = jnp.exp(s - jnp.max(s, axis=-1, keepdims=True))
            o = _dot(p.astype(BF16), vh) * (1.0 / jnp.sum(p, axis=-1, keepdims=True))
            oh_ref[r0:r0 + tpb, c0:c0 + X_HEAD_DIM] = o.astype(BF16)
    o_ref[...] = x + _dot(oh_ref[...], wo_ref[...])


def _mix_xattn(x_all, row0, mix, l, w_out, norm, wq, wo, mk, mv, lmem, *, nb, t_len, tm):
    tpb = min(t_len, tm)
    nbs = tm // tpb
    spb = t_len // tpb
    steps = nb * t_len // tm
    blk0 = row0 // tm
    par = lambda shape: pl.BlockSpec((None,) + shape, lambda i: (l,) + (0,) * len(shape))
    mem_spec = pl.BlockSpec((None, nbs, N_MEM, X_WIDTH), lambda i: (lmem, i // spb, 0, 0))
    x_spec = pl.BlockSpec((tm, D_MODEL), lambda i: (blk0 + i, 0))
    return pl.pallas_call(
        functools.partial(_mix_xattn_body, nbs=nbs, tpb=tpb),
        grid=(steps,),
        in_specs=[x_spec] + [pl.BlockSpec((tm, 256), lambda i: (i, 0))] * 4 + [
            par((D_MODEL, D_MODEL)), par((1, D_MODEL)), par((D_MODEL, X_WIDTH)), par((X_WIDTH, D_MODEL)),
            mem_spec, mem_spec],
        out_specs=x_spec,
        out_shape=jax.ShapeDtypeStruct(x_all.shape, F32),
        scratch_shapes=[pltpu.VMEM((tm, X_WIDTH), BF16)],
        input_output_aliases={0: 0},
        name="mix_xattn",
        compiler_params=_cparams("arbitrary"),
    )(x_all, *mix, w_out, norm, wq, wo, mk, mv)


def _token_tile(n, cap):
    t = cap
    while n % t:
        t //= 2
    return t


def kernel(x_prompt, x_sample, mem_prompt, cache_attn_k, cache_attn_v, state_ssm_re, state_ssm_im, state_conv, cache_mem_k, cache_mem_v, ffn1_norm, ffn1_w_gate, ffn1_w_up, ffn1_w_down, mix_norm, w_in, w_out, ssm_a_re, ssm_a_im, ssm_b_re, ssm_b_im, ssm_c_re, ssm_c_im, ssm_d, ssm_log_dt, ssm_w_glu, ssm_b_glu, dattn_lq1, dattn_lk1, dattn_lq2, dattn_lk2, dattn_norm, conv_w, conv_b, conv_ln_g, conv_ln_b, conv_w_pw, gmlp_ln_g, gmlp_ln_b, gmlp_ws, gmlp_bs, xattn_norm, mem_norm, xattn_wq, xattn_wk, xattn_wv, xattn_wo, ffn2_norm, ffn2_w_gate, ffn2_w_up, ffn2_w_down, final_norm):
    depth = w_in.shape[0]
    bp, tp, _ = x_prompt.shape
    bs, ts, _ = x_sample.shape
    past = cache_attn_k.shape[2]
    n_p, n_s = bp * tp, bs * ts
    n = n_p + n_s
    tm = _token_tile(math.gcd(n_p, n_s), 1024)
    tm_x = _token_tile(tp, 1024)

    bf = lambda w: w.astype(BF16)
    row = lambda v: v.reshape(depth, 1, -1)
    ffn1 = (row(ffn1_norm), bf(ffn1_w_gate), bf(ffn1_w_up), bf(ffn1_w_down))
    ffn2 = (row(ffn2_norm), bf(ffn2_w_gate), bf(ffn2_w_up), bf(ffn2_w_down))
    w_in_b, w_out_b = bf(w_in), bf(w_out)
    wq_b, wk_b, wv_b, wo_b = bf(xattn_wq), bf(xattn_wk), bf(xattn_wv), bf(xattn_wo)
    wpw_b = bf(conv_w_pw)
    lq = jnp.stack([dattn_lq1, dattn_lq2], axis=1)
    lk = jnp.stack([dattn_lk1, dattn_lk2], axis=1)
    gh = jnp.tile(dattn_norm, (1, LANES // ATT_HEAD_DIM)).reshape(depth, 1, LANES)
    bs_wide = jnp.repeat(jnp.swapaxes(gmlp_bs, 1, 2), 256 // GMLP_HEADS, axis=2)
    final_g = final_norm.reshape(1, D_MODEL)
    cache_k = cache_attn_k.reshape(depth, bs, past, 256)
    cache_v = cache_attn_v.reshape(depth, bs, past, 256)
    mem_k_s = cache_mem_k.reshape(depth, bs, N_MEM, X_WIDTH)
    mem_v_s = cache_mem_v.reshape(depth, bs, N_MEM, X_WIDTH)
    zero_state = jnp.zeros((bp, 2 * SSM_COLS), F32)
    zero_buf = jnp.zeros((bp, CONV_K - 1, 256), F32)

    x = jnp.concatenate([x_prompt.reshape(n_p, D_MODEL), x_sample.reshape(n_s, D_MODEL)], axis=0)
    outs = {name: [] for name in ("pk", "pv", "psre", "psim", "pconv", "pmk", "pmv",
                                  "sk", "sv", "ssre", "ssim", "sconv", "sgv")}
    for l in range(depth):
        lam_init = 0.8 - 0.6 * math.exp(-0.3 * l)
        x = _ffn(x, l, *ffn1, final_g, final=False, tm=tm, tf=512)
        u, q, k, v, pc, pg = _proj_in(x, l, row(mix_norm), w_in_b, tm=tm)
        s5p = _s5_params(ssm_a_re[l], ssm_a_im[l], ssm_b_re[l], ssm_b_im[l], ssm_c_re[l], ssm_c_im[l],
                         ssm_d[l], ssm_log_dt[l], ssm_w_glu[l], ssm_b_glu[l])
        s0_s = jnp.concatenate([state_ssm_re[l].reshape(bs, SSM_COLS), state_ssm_im[l].reshape(bs, SSM_COLS)], axis=1)
        a_p, st_p = _s5(u, 0, zero_state, s5p, nb=bp, t_len=tp, chained=True)
        b_p = _attn_prompt(q, k, v, l, lq, lk, gh, lam_init, nb=bp, t_len=tp, tq=min(tp, 256))
        c_p, buf_p = _conv(pc, 0, zero_buf, l, conv_w, row(conv_b), row(conv_ln_g), row(conv_ln_b), wpw_b,
                           nb=bp, t_len=tp)
        d_p, _ = _gmlp(pg, 0, l, row(gmlp_ln_g), row(gmlp_ln_b), gmlp_ws, bs_wide, nb=bp, t_len=tp)
        mk_p, mv_p = _memkv(mem_prompt, l, row(mem_norm), wk_b, wv_b)
        a_s, st_s = _s5(u, n_p, s0_s, s5p, nb=bs, t_len=ts, chained=False)
        b_s = _attn_sample(q, k, v, n_p, cache_k, cache_v, l, lq, lk, gh, lam_init, nb=bs, t_len=ts)
        c_s, buf_s = _conv(pc, n_p, state_conv[l], l, conv_w, row(conv_b), row(conv_ln_g), row(conv_ln_b), wpw_b,
                           nb=bs, t_len=ts)
        d_s, gv_s = _gmlp(pg, n_p, l, row(gmlp_ln_g), row(gmlp_ln_b), gmlp_ws, bs_wide, nb=bs, t_len=ts)
        x = _mix_xattn(x, 0, (a_p, b_p, c_p, d_p), l, w_out_b, row(xattn_norm), wq_b, wo_b, mk_p[None], mv_p[None], 0,
                       nb=bp, t_len=tp, tm=tm_x)
        x = _mix_xattn(x, n_p, (a_s, b_s, c_s, d_s), l, w_out_b, row(xattn_norm), wq_b, wo_b,
                       mem_k_s, mem_v_s, l, nb=bs, t_len=ts, tm=_token_tile(n_s, 512))
        x = _ffn(x, l, *ffn2, final_g, final=(l == depth - 1), tm=tm, tf=512)

        outs["pk"].append(k[:n_p].reshape(bp, tp, ATT_HEADS, ATT_HEAD_DIM))
        outs["pv"].append(v[:n_p].reshape(bp, tp, ATT_HEADS, ATT_HEAD_DIM))
        outs["psre"].append(st_p[:, :SSM_COLS].reshape(bp, SSM_GROUPS, SSM_STATE))
        outs["psim"].append(st_p[:, SSM_COLS:].reshape(bp, SSM_GROUPS, SSM_STATE))
        outs["pconv"].append(buf_p)
        outs["pmk"].append(mk_p.reshape(bp, N_MEM, X_HEADS, X_HEAD_DIM))
        outs["pmv"].append(mv_p.reshape(bp, N_MEM, X_HEADS, X_HEAD_DIM))
        outs["sk"].append(k[n_p:].reshape(bs, ts, ATT_HEADS, ATT_HEAD_DIM))
        outs["sv"].append(v[n_p:].reshape(bs, ts, ATT_HEADS, ATT_HEAD_DIM))
        outs["ssre"].append(st_s[:, :SSM_COLS].reshape(bs, SSM_GROUPS, SSM_STATE))
        outs["ssim"].append(st_s[:, SSM_COLS:].reshape(bs, SSM_GROUPS, SSM_STATE))
        outs["sconv"].append(buf_s)
        outs["sgv"].append(gv_s.reshape(bs, ts, 256))

    st = {name: jnp.stack(vals) for name, vals in outs.items()}
    return (x[:n_p].reshape(bp, tp, D_MODEL), x[n_p:].reshape(bs, ts, D_MODEL),
            st["pk"], st["pv"], st["psre"], st["psim"], st["pconv"], st["pmk"], st["pmv"],
            st["sk"], st["sv"], st["ssre"], st["ssim"], st["sconv"], st["sgv"])
```

```python
import functools
import math

import jax
import jax.numpy as jnp
import numpy as np
from jax import lax
from jax.experimental import pallas as pl
from jax.experimental.pallas import tpu as pltpu

F32 = jnp.float32
BF16 = jnp.bfloat16
EPS = 1e-6

D_MODEL = 1024
D_FF = 4096
DEPTH = 4
CHUNK_SHIFT = 6
N_MEM = 256
SSM_WIDTH = 256
SSM_GROUP = 16
SSM_GROUPS = 16
SSM_STATE = 64
SSM_COLS = SSM_GROUPS * SSM_STATE
SSM_TILES = SSM_COLS // 128
ATT_HEADS = 4
ATT_HEAD_DIM = 64
ATT_HALF = 32
ATT_GROUPS = 2 * ATT_HEADS
CONV_K = 31
GMLP_HEADS = 4
GMLP_CHUNK = 128
X_HEADS = 4
X_HEAD_DIM = 128
X_WIDTH = 512
LANES = 128
SUBLANES = 8
VMEM_LIMIT = 56 * 1024 * 1024


def _cparams(*sem):
    return pltpu.CompilerParams(dimension_semantics=sem, vmem_limit_bytes=VMEM_LIMIT)


def _rms(x, g):
    return x * lax.rsqrt(jnp.mean(x * x, axis=-1, keepdims=True) + EPS) * g


def _layer_norm(x, g, b):
    mu = jnp.mean(x, axis=-1, keepdims=True)
    xc = x - mu
    var = jnp.mean(xc * xc, axis=-1, keepdims=True)
    return xc * lax.rsqrt(var + EPS) * g + b


def _gelu(x):
    return 0.5 * x * (1.0 + jnp.tanh(math.sqrt(2.0 / math.pi) * (x + 0.044715 * (x * x * x))))


def _sigmoid(x):
    return 1.0 / (1.0 + jnp.exp(-x))


def _dot(a, b):
    return jnp.dot(a, b, preferred_element_type=F32)


def _dot_nt(a, b):
    return lax.dot_general(a, b, (((1,), (1,)), ((), ())), preferred_element_type=F32)


def _ffn_body(x_ref, g_ref, wg_ref, wu_ref, wd_ref, fg_ref, o_ref, h_ref, acc_ref, *, nj, final):
    j = pl.program_id(1)

    @pl.when(j == 0)
    def _():
        h_ref[...] = _rms(x_ref[...], g_ref[...]).astype(BF16)
        acc_ref[...] = jnp.zeros_like(acc_ref)

    h = h_ref[...]
    a = _dot(h, wg_ref[...])
    b = _dot(h, wu_ref[...])
    act = (a * _sigmoid(a) * b).astype(BF16)
    acc_ref[...] += _dot(act, wd_ref[...])

    @pl.when(j == nj - 1)
    def _():
        y = x_ref[...] + 0.5 * acc_ref[...]
        if final:
            y = _rms(y, fg_ref[...])
        o_ref[...] = y


def _ffn(x, l, norm, wg, wu, wd, final_g, *, final, tm, tf):
    n = x.shape[0]
    nj = D_FF // tf
    return pl.pallas_call(
        functools.partial(_ffn_body, nj=nj, final=final),
        grid=(n // tm, nj),
        in_specs=[
            pl.BlockSpec((tm, D_MODEL), lambda i, j: (i, 0)),
            pl.BlockSpec((None, 1, D_MODEL), lambda i, j: (l, 0, 0)),
            pl.BlockSpec((None, D_MODEL, tf), lambda i, j: (l, 0, j)),
            pl.BlockSpec((None, D_MODEL, tf), lambda i, j: (l, 0, j)),
            pl.BlockSpec((None, tf, D_MODEL), lambda i, j: (l, j, 0)),
            pl.BlockSpec((1, D_MODEL), lambda i, j: (0, 0)),
        ],
        out_specs=pl.BlockSpec((tm, D_MODEL), lambda i, j: (i, 0)),
        out_shape=jax.ShapeDtypeStruct((n, D_MODEL), F32),
        scratch_shapes=[pltpu.VMEM((tm, D_MODEL), BF16), pltpu.VMEM((tm, D_MODEL), F32)],
        name="ffn",
        compiler_params=_cparams("parallel", "arbitrary"),
    )(x, norm, wg, wu, wd, final_g)


PROJ_SPLITS = (("u0", 0, 128), ("u1", 128, 128), ("q", 256, 256), ("k", 512, 256), ("v", 768, 256),
               ("c", 1024, 512), ("g", 1536, 512))


def _proj_body(x_ref, g_ref, w_ref, *o_refs):
    h = _rms(x_ref[...], g_ref[...]).astype(BF16)
    for (_, off, width), o_ref in zip(PROJ_SPLITS, o_refs):
        o_ref[...] = _dot(h, w_ref[:, off:off + width])


def _proj_in(x, l, norm, w_in, *, tm):
    n = x.shape[0]
    return pl.pallas_call(
        _proj_body,
        grid=(n // tm,),
        in_specs=[
            pl.BlockSpec((tm, D_MODEL), lambda i: (i, 0)),
            pl.BlockSpec((None, 1, D_MODEL), lambda i: (l, 0, 0)),
            pl.BlockSpec((None, D_MODEL, 2048), lambda i: (l, 0, 0)),
        ],
        out_specs=[pl.BlockSpec((tm, w), lambda i: (i, 0)) for _, _, w in PROJ_SPLITS],
        out_shape=[jax.ShapeDtypeStruct((n, w), F32) for _, _, w in PROJ_SPLITS],
        name="proj_in",
        compiler_params=_cparams("parallel"),
    )(x, norm, w_in)


def _cmul(ar, ai, br, bi):
    return ar * br - ai * bi, ar * bi + ai * br


def _s5_body(u0_ref, u1_ref, s0_ref, ab_ref, bblk_ref, cblk_ref, d_ref, wglu_ref, bglu_ref,
             y0_ref, y1_ref, sout_ref, up_ref, w_ref, carry_ref, e_ref, sin_ref, *, lc, nt, chained, tpc):
    t = pl.program_id(1)
    for j in range(lc):
        up_ref[SUBLANES * j:SUBLANES * (j + 1), 0:LANES] = u0_ref[pl.ds(j, SUBLANES, stride=lc), :]
        up_ref[SUBLANES * j:SUBLANES * (j + 1), LANES:2 * LANES] = u1_ref[pl.ds(j, SUBLANES, stride=lc), :]
    u = up_ref[...]
    bu = _dot(u.astype(BF16), bblk_ref[...])
    for c in range(2 * SSM_TILES):
        w_ref[c] = bu[:, c * LANES:(c + 1) * LANES]

    if chained:
        @pl.when(t == 0)
        def _():
            carry_ref[...] = s0_ref[...]

    def rows(j):
        return slice(SUBLANES * j, SUBLANES * (j + 1))

    def cols(c):
        return slice(c * LANES, (c + 1) * LANES)

    for c0 in range(0, SSM_TILES, tpc):
        re_t = list(range(c0, c0 + tpc))
        im_t = [SSM_TILES + c for c in re_t]
        ar = [jnp.broadcast_to(ab_ref[0:1, cols(c)], (SUBLANES, LANES)) for c in re_t]
        ai = [jnp.broadcast_to(ab_ref[1:2, cols(c)], (SUBLANES, LANES)) for c in re_t]
        if chained:
            zr = [jnp.zeros((SUBLANES, LANES), F32)] * tpc
            zi = [jnp.zeros((SUBLANES, LANES), F32)] * tpc
        else:
            zr = [s0_ref[:, cols(c)] for c in re_t]
            zi = [s0_ref[:, cols(c)] for c in im_t]
        for j in range(lc):
            for n in range(tpc):
                pr, pi = _cmul(ar[n], ai[n], zr[n], zi[n])
                zr[n] = pr + w_ref[re_t[n], rows(j), :]
                zi[n] = pi + w_ref[im_t[n], rows(j), :]
                w_ref[re_t[n], rows(j), :] = zr[n]
                w_ref[im_t[n], rows(j), :] = zi[n]
        if not chained:
            for n in range(tpc):
                sout_ref[:, cols(re_t[n])] = zr[n]
                sout_ref[:, cols(im_t[n])] = zi[n]
            continue
        for n in range(tpc):
            e_ref[:, cols(re_t[n])] = zr[n]
            e_ref[:, cols(im_t[n])] = zi[n]
        lo, hi = c0 * LANES, (c0 + tpc) * LANES
        re_c, im_c = slice(lo, hi), slice(SSM_COLS + lo, SSM_COLS + hi)
        pr, pi = ab_ref[0:1, re_c], ab_ref[1:2, re_c]
        for _ in range(int(math.log2(lc))):
            pr, pi = _cmul(pr, pi, pr, pi)
        fr = carry_ref[0:1, re_c]
        fi = carry_ref[0:1, im_c]
        for s in range(SUBLANES):
            sin_ref[s:s + 1, re_c] = fr
            sin_ref[s:s + 1, im_c] = fi
            gr, gi = _cmul(pr, pi, fr, fi)
            fr = gr + e_ref[s:s + 1, re_c]
            fi = gi + e_ref[s:s + 1, im_c]
        carry_ref[0:1, re_c] = fr
        carry_ref[0:1, im_c] = fi
        cr = [sin_ref[:, cols(c)] for c in re_t]
        ci = [sin_ref[:, cols(c)] for c in im_t]
        for j in range(lc):
            for n in range(tpc):
                cr[n], ci[n] = _cmul(ar[n], ai[n], cr[n], ci[n])
                w_ref[re_t[n], rows(j), :] = w_ref[re_t[n], rows(j), :] + cr[n]
                w_ref[im_t[n], rows(j), :] = w_ref[im_t[n], rows(j), :] + ci[n]

    if chained:
        @pl.when(t == nt - 1)
        def _():
            sout_ref[...] = carry_ref[...]

    states = jnp.concatenate([w_ref[c].astype(BF16) for c in range(2 * SSM_TILES)], axis=1)
    y = _dot(states, cblk_ref[...]) + d_ref[...] * u
    g = _gelu(y)
    up_ref[...] = g * _sigmoid(_dot(g.astype(BF16), wglu_ref[...]) + bglu_ref[...])
    for j in range(lc):
        y0_ref[pl.ds(j, SUBLANES, stride=lc), :] = up_ref[rows(j), 0:LANES]
        y1_ref[pl.ds(j, SUBLANES, stride=lc), :] = up_ref[rows(j), LANES:2 * LANES]


def _s5(u0_all, u1_all, row0, s0, prm, *, nb, t_len, chained):
    if chained:
        tt = min(t_len, 256)
        nt = t_len // tt
        grid = (nb, nt)
        lc = tt // SUBLANES
        blk0 = row0 // tt
        u_map = lambda b, t: (blk0 + b * nt + t, 0)
        y_map = lambda b, t: (b * nt + t, 0)
        s_spec = pl.BlockSpec((None, 1, 2 * SSM_COLS), lambda b, t: (b, 0, 0))
        s0 = s0.reshape(nb, 1, 2 * SSM_COLS)
        s_shape = jax.ShapeDtypeStruct((nb, 1, 2 * SSM_COLS), F32)
    else:
        tt = SUBLANES * t_len
        nt = 1
        grid = (nb // SUBLANES, 1)
        lc = t_len
        blk0 = row0 // tt
        u_map = lambda b, t: (blk0 + b, 0)
        y_map = lambda b, t: (b, 0)
        s_spec = pl.BlockSpec((SUBLANES, 2 * SSM_COLS), lambda b, t: (b, 0))
        s_shape = jax.ShapeDtypeStruct((nb, 2 * SSM_COLS), F32)
    full = lambda shape: pl.BlockSpec(shape, lambda b, t: (0,) * len(shape))
    y0, y1, s_out = pl.pallas_call(
        functools.partial(_s5_body, lc=lc, nt=nt, chained=chained, tpc=4),
        grid=grid,
        in_specs=[
            pl.BlockSpec((tt, LANES), u_map),
            pl.BlockSpec((tt, LANES), u_map),
            s_spec,
            full((2, SSM_COLS)),
            full((SSM_WIDTH, 2 * SSM_COLS)),
            full((2 * SSM_COLS, SSM_WIDTH)),
            full((1, SSM_WIDTH)),
            full((SSM_WIDTH, SSM_WIDTH)),
            full((1, SSM_WIDTH)),
        ],
        out_specs=[pl.BlockSpec((tt, LANES), y_map), pl.BlockSpec((tt, LANES), y_map), s_spec],
        out_shape=[jax.ShapeDtypeStruct((nb * t_len, LANES), F32)] * 2 + [s_shape],
        scratch_shapes=[
            pltpu.VMEM((tt, SSM_WIDTH), F32),
            pltpu.VMEM((2 * SSM_TILES, tt, LANES), F32),
            pltpu.VMEM((1, 2 * SSM_COLS), F32),
            pltpu.VMEM((SUBLANES, 2 * SSM_COLS), F32),
            pltpu.VMEM((SUBLANES, 2 * SSM_COLS), F32),
        ],
        name="s5_chained" if chained else "s5_batched",
        compiler_params=_cparams("parallel", "arbitrary"),
    )(u0_all, u1_all, s0, prm["ab"], prm["bblk"], prm["cblk"], prm["d"], prm["wglu"], prm["bglu"])
    return (y0, y1), s_out.reshape(nb, 2 * SSM_COLS)


def _s5_params(a_re, a_im, b_re, b_im, c_re, c_im, d, log_dt, w_glu, b_glu):
    dt = jnp.exp(log_dt)[:, None]
    mag = jnp.exp(a_re * dt)
    abr, abi = mag * jnp.cos(a_im * dt), mag * jnp.sin(a_im * dt)
    den = a_re * a_re + a_im * a_im
    cr = ((abr - 1.0) * a_re + abi * a_im) / den
    ci = (abi * a_re - (abr - 1.0) * a_im) / den
    bbr = cr[..., None] * b_re - ci[..., None] * b_im
    bbi = cr[..., None] * b_im + ci[..., None] * b_re
    eye = jnp.eye(SSM_GROUPS, dtype=F32)

    def blk_in(m):
        return jnp.einsum("gpc,gh->gchp", m, eye).reshape(SSM_WIDTH, SSM_COLS)

    def blk_out(m):
        return jnp.einsum("gcp,gh->gphc", m, eye).reshape(SSM_COLS, SSM_WIDTH)

    return dict(
        ab=jnp.stack([abr.reshape(-1), abi.reshape(-1)]),
        bblk=jnp.concatenate([blk_in(bbr), blk_in(bbi)], axis=1).astype(BF16),
        cblk=jnp.concatenate([blk_out(c_re), -blk_out(c_im)], axis=0).astype(BF16),
        d=d.reshape(1, SSM_WIDTH),
        wglu=w_glu.astype(BF16),
        bglu=b_glu.reshape(1, SSM_WIDTH),
    )


def _alibi_slope(h):
    return 2.0 ** (-8.0 * (h + 1) / ATT_HEADS)


def _stack_queries(q, qs_ref):
    lane = lax.broadcasted_iota(jnp.int32, q.shape, 1)
    qsc = q * (ATT_HALF ** -0.5)
    for g in range(ATT_GROUPS):
        lo = g * ATT_HALF
        qs_ref[g] = jnp.where((lane >= lo) & (lane < lo + ATT_HALF), qsc, 0.0).astype(BF16)


def _attn_finish(acc, linv, lq_ref, lk_ref, gh_ref, lam_init):
    lam = (jnp.exp(jnp.sum(lq_ref[0:1, :] * lk_ref[0:1, :], axis=-1, keepdims=True))
           - jnp.exp(jnp.sum(lq_ref[1:2, :] * lk_ref[1:2, :], axis=-1, keepdims=True)) + lam_init)
    outs = []
    for pair in range(ATT_HEADS // 2):
        os_ = []
        for h in (2 * pair, 2 * pair + 1):
            os_.append(acc(2 * h) * linv(2 * h) - lam * (acc(2 * h + 1) * linv(2 * h + 1)))
        lane = lax.broadcasted_iota(jnp.int32, os_[0].shape, 1)
        first = lane < ATT_HEAD_DIM
        o = jnp.where(first, os_[0], os_[1])
        sq = o * o
        ss_all = jnp.sum(sq, axis=-1, keepdims=True)
        ss_a = jnp.sum(jnp.where(first, sq, 0.0), axis=-1, keepdims=True)
        ms = jnp.where(first, ss_a, ss_all - ss_a) * (1.0 / ATT_HEAD_DIM)
        outs.append(o * lax.rsqrt(ms + EPS) * gh_ref[...] * (1.0 - lam_init))
    return jnp.concatenate(outs, axis=-1)


ATT_POS_LANES = 8
ATT_MASK_NEG = -1e30
LOG2E = math.log2(math.e)


def _alibi_feature_rows():
    rows = np.zeros((ATT_HEADS, LANES), np.float32)
    for h in range(ATT_HEADS):
        rest = _alibi_slope(h) * LOG2E
        for t in range(ATT_POS_LANES // 2):
            piece = float(np.float32(rest).astype(jnp.bfloat16))
            rows[h, t] = piece * (1 << CHUNK_SHIFT)
            rows[h, ATT_POS_LANES // 2 + t] = piece
            rest -= piece
    return rows


def _augment_keys(k_ref, v_ref, kaug_ref, vaug_ref, t_len):
    rc = min(t_len, 512)
    lane = lax.broadcasted_iota(jnp.int32, (rc, LANES), 1)
    for r0 in range(0, t_len, rc):
        pos = r0 + lax.broadcasted_iota(jnp.int32, (rc, LANES), 0)
        chunk = jnp.right_shift(pos, CHUNK_SHIFT)
        offset = jnp.bitwise_and(pos, (1 << CHUNK_SHIFT) - 1)
        onehot = jnp.where(lane - ATT_POS_LANES == chunk, 1, 0)
        feat = jnp.where(lane < ATT_POS_LANES // 2, chunk, jnp.where(lane < ATT_POS_LANES, offset, onehot))
        feat = feat.astype(F32).astype(BF16)
        rows = slice(r0, r0 + rc)
        for p in range(ATT_HEADS // 2):
            kaug_ref[p, rows, 0:LANES] = k_ref[rows, p * LANES:(p + 1) * LANES].astype(BF16)
            kaug_ref[p, rows, LANES:2 * LANES] = feat
            vt = v_ref[rows, p * LANES:(p + 1) * LANES]
            vaug_ref[2 * p, rows, :] = jnp.where(
                lane < ATT_HEAD_DIM, vt, jnp.where(lane == ATT_HEAD_DIM, 1.0, 0.0)).astype(BF16)
            vaug_ref[2 * p + 1, rows, :] = jnp.where(
                lane >= ATT_HEAD_DIM, vt, jnp.where(lane == 0, 1.0, 0.0)).astype(BF16)


def _augment_queries(q, q0, feat_ref, qaug_ref, tq):
    lane = lax.broadcasted_iota(jnp.int32, (tq, LANES), 1)
    qchunk = jnp.right_shift(q0 + lax.broadcasted_iota(jnp.int32, (tq, LANES), 0), CHUNK_SHIFT)
    kchunk = lane - ATT_POS_LANES
    first_chunk = jnp.right_shift(q0, CHUNK_SHIFT)
    masks = (jnp.where((kchunk >= 0) & (kchunk >= first_chunk), ATT_MASK_NEG, 0.0),
             jnp.where((kchunk >= 0) & (kchunk > qchunk), ATT_MASK_NEG, 0.0))
    for p in range(ATT_HEADS // 2):
        qt = q[:, p * LANES:(p + 1) * LANES] * (ATT_HALF ** -0.5 * LOG2E)
        for hh in range(2):
            feats = [(feat_ref[2 * p + hh:2 * p + hh + 1, :] + mk).astype(BF16) for mk in masks]
            for half in range(2):
                lo = hh * ATT_HEAD_DIM + half * ATT_HALF
                rows = slice((hh * 2 + half) * tq, (hh * 2 + half + 1) * tq)
                qm = jnp.where((lane >= lo) & (lane < lo + ATT_HALF), qt, 0.0).astype(BF16)
                for variant in range(2):
                    qaug_ref[variant, p, rows, 0:LANES] = qm
                    qaug_ref[variant, p, rows, LANES:2 * LANES] = feats[variant]


def _softmax_update(variant, k0, width, fix, qaug_ref, kaug_ref, vaug_ref, m_ref, acc_ref, tq):
    for p in range(ATT_HEADS // 2):
        kt = kaug_ref[p, pl.ds(k0, width), :]
        for gi in range(4):
            hh = gi // 2
            rows = slice(gi * tq, (gi + 1) * tq)
            s = _dot_nt(qaug_ref[variant, p, rows, :], kt)
            if fix is not None:
                s = s + fix(2 * p + hh)
            m_prev = m_ref[p, rows, :]
            m_new = jnp.maximum(m_prev, jnp.max(s, axis=-1, keepdims=True))
            alpha = jnp.exp2(m_prev - m_new)
            pb = jnp.exp2(s - jnp.concatenate([m_new] * (width // LANES), axis=1)).astype(BF16)
            m_ref[p, rows, :] = m_new
            acc_ref[p, rows, :] = alpha * acc_ref[p, rows, :] + _dot(
                pb, vaug_ref[2 * p + hh, pl.ds(k0, width), :])


def _attn_prompt_body(q_ref, k_ref, v_ref, feat_ref, lq_ref, lk_ref, gh_ref, o_ref,
                      kaug_ref, vaug_ref, qaug_ref, m_ref, acc_ref, *, tq, tk, t_len, lam_init):
    i = pl.program_id(1)

    @pl.when(i == 0)
    def _():
        _augment_keys(k_ref, v_ref, kaug_ref, vaug_ref, t_len)

    _augment_queries(q_ref[...], i * tq, feat_ref, qaug_ref, tq)
    m_ref[...] = jnp.full_like(m_ref, -jnp.inf)
    acc_ref[...] = jnp.zeros_like(acc_ref)

    def past_tile(c, carry):
        _softmax_update(0, pl.multiple_of(c * tk, tk), tk, None, qaug_ref, kaug_ref, vaug_ref, m_ref, acc_ref, tq)
        return carry

    lax.fori_loop(0, (i * tq + tk - 1) // tk, past_tile, 0)

    r = lax.broadcasted_iota(jnp.int32, (tq, tq), 0)
    c = lax.broadcasted_iota(jnp.int32, (tq, tq), 1)
    ahead = jnp.where(jnp.right_shift(r, CHUNK_SHIFT) == jnp.right_shift(c, CHUNK_SHIFT),
                      jnp.maximum(c - r, 0), 0).astype(F32)
    _softmax_update(1, pl.multiple_of(i * tq, tq), tq, lambda h: ahead * (-2.0 * LOG2E * _alibi_slope(h)),
                    qaug_ref, kaug_ref, vaug_ref, m_ref, acc_ref, tq)

    def acc(g):
        h, half = divmod(g, 2)
        p, hh = divmod(h, 2)
        return acc_ref[p, (hh * 2 + half) * tq:(hh * 2 + half + 1) * tq, :]

    def linv(g):
        ones_lane = ATT_HEAD_DIM if (g // 2) % 2 == 0 else 0
        return 1.0 / acc(g)[:, ones_lane:ones_lane + 1]

    o_ref[...] = _attn_finish(acc, linv, lq_ref, lk_ref, gh_ref, lam_init)


def _attn_prompt(q_all, k_all, v_all, l, lq, lk, gh, lam_init, *, nb, t_len, tq, tk):
    assert t_len % tk == 0 and tk % tq == 0 and (t_len >> CHUNK_SHIFT) <= LANES - ATT_POS_LANES
    nq = t_len // tq
    q_map = lambda b, i: (b * nq + i, 0)
    kv_spec = pl.BlockSpec((t_len, 256), lambda b, i: (b, 0))
    par = lambda shape: pl.BlockSpec((None,) + shape, lambda b, i: (l, 0, 0))
    return pl.pallas_call(
        functools.partial(_attn_prompt_body, tq=tq, tk=tk, t_len=t_len, lam_init=lam_init),
        grid=(nb, nq),
        in_specs=[
            pl.BlockSpec((tq, 256), q_map), kv_spec, kv_spec,
            pl.BlockSpec((ATT_HEADS, LANES), lambda b, i: (0, 0)),
            par((2, ATT_HALF)), par((2, ATT_HALF)), par((1, LANES)),
        ],
        out_specs=pl.BlockSpec((tq, 256), q_map),
        out_shape=jax.ShapeDtypeStruct((nb * t_len, 256), F32),
        scratch_shapes=[
            pltpu.VMEM((ATT_HEADS // 2, t_len, 2 * LANES), BF16),
            pltpu.VMEM((ATT_HEADS, t_len, LANES), BF16),
            pltpu.VMEM((2, ATT_HEADS // 2, 4 * tq, 2 * LANES), BF16),
            pltpu.VMEM((ATT_HEADS // 2, 4 * tq, LANES), F32),
            pltpu.VMEM((ATT_HEADS // 2, 4 * tq, LANES), F32),
        ],
        name="attn_prompt",
        compiler_params=_cparams("parallel", "arbitrary"),
    )(q_all, k_all, v_all, jnp.asarray(_alibi_feature_rows()), lq, lk, gh)


def _attn_sample_body(q_ref, kn_ref, vn_ref, kc_ref, vc_ref, lq_ref, lk_ref, gh_ref, o_ref,
                      qs_ref, acc_ref, linv_ref, *, tq, past, lam_init):
    _stack_queries(q_ref[...], qs_ref)
    kc = kc_ref[...].astype(BF16)
    vc = vc_ref[...].astype(BF16)
    kn = kn_ref[...].astype(BF16)
    vn = vn_ref[...].astype(BF16)
    r_c = lax.broadcasted_iota(jnp.int32, (tq, past), 0)
    c_c = lax.broadcasted_iota(jnp.int32, (tq, past), 1)
    nd_c = -(past + r_c - c_c).astype(F32)
    r_n = lax.broadcasted_iota(jnp.int32, (tq, tq), 0)
    c_n = lax.broadcasted_iota(jnp.int32, (tq, tq), 1)
    nd_n = -jnp.abs(r_n - c_n).astype(F32)
    for g in range(ATT_GROUPS):
        h = g // 2
        s_c = _dot_nt(qs_ref[g], kc) + _alibi_slope(h) * nd_c
        s_n = _dot_nt(qs_ref[g], kn) + _alibi_slope(h) * nd_n
        m = jnp.maximum(jnp.max(s_c, axis=-1, keepdims=True), jnp.max(s_n, axis=-1, keepdims=True))
        p_c = jnp.exp(s_c - m)
        p_n = jnp.exp(s_n - m)
        lsum = jnp.sum(p_c, axis=-1, keepdims=True) + jnp.sum(p_n, axis=-1, keepdims=True)
        vcol = (h // 2) * LANES
        acc_ref[g] = (_dot(p_c.astype(BF16), vc[:, vcol:vcol + LANES])
                      + _dot(p_n.astype(BF16), vn[:, vcol:vcol + LANES]))
        linv_ref[g] = jnp.broadcast_to(1.0 / lsum, (tq, LANES))
    o_ref[...] = _attn_finish(lambda g: acc_ref[g], lambda g: linv_ref[g][:, 0:1],
                              lq_ref, lk_ref, gh_ref, lam_init)


def _attn_sample(q_all, k_all, v_all, row0, cache_k, cache_v, l, lq, lk, gh, lam_init, *, nb, t_len):
    past = cache_k.shape[2]
    blk0 = row0 // t_len
    new_map = lambda b: (blk0 + b, 0)
    par = lambda shape: pl.BlockSpec((None,) + shape, lambda b: (l, 0, 0))
    cache_spec = pl.BlockSpec((None, None, past, 256), lambda b: (l, b, 0, 0))
    return pl.pallas_call(
        functools.partial(_attn_sample_body, tq=t_len, past=past, lam_init=lam_init),
        grid=(nb,),
        in_specs=[
            pl.BlockSpec((t_len, 256), new_map),
            pl.BlockSpec((t_len, 256), new_map),
            pl.BlockSpec((t_len, 256), new_map),
            cache_spec, cache_spec,
            par((2, ATT_HALF)), par((2, ATT_HALF)), par((1, LANES)),
        ],
        out_specs=pl.BlockSpec((t_len, 256), lambda b: (b, 0)),
        out_shape=jax.ShapeDtypeStruct((nb * t_len, 256), F32),
        scratch_shapes=[
            pltpu.VMEM((ATT_GROUPS, t_len, 256), BF16),
            pltpu.VMEM((ATT_GROUPS, t_len, LANES), F32),
            pltpu.VMEM((ATT_GROUPS, t_len, LANES), F32),
        ],
        name="attn_sample",
        compiler_params=_cparams("parallel"),
    )(q_all, k_all, v_all, cache_k, cache_v, lq, lk, gh)


CONV_PAD = 32
CONV_ROWS = 64


def _conv_body(p_ref, buf_ref, w_ref, b_ref, lg_ref, lb_ref, wpw_ref, o_ref, nbuf_ref, zp_ref, *, tt, nt):
    t = pl.program_id(1)
    off = CONV_PAD - (CONV_K - 1)

    @pl.when(t == 0)
    def _():
        zp_ref[0:off, :] = jnp.zeros((off, 256), F32)
        zp_ref[off:CONV_PAD, :] = buf_ref[...]

    @pl.when(t > 0)
    def _():
        zp_ref[0:CONV_PAD, :] = zp_ref[tt:tt + CONV_PAD, :]

    p = p_ref[...]
    zp_ref[CONV_PAD:CONV_PAD + tt, :] = p[:, 0:256] * _sigmoid(p[:, 256:512])
    for r0 in range(0, tt, CONV_ROWS):
        acc = jnp.broadcast_to(b_ref[...], (CONV_ROWS, 256))
        for k in range(CONV_K):
            acc = acc + zp_ref[r0 + off + k:r0 + off + k + CONV_ROWS, :] * w_ref[k:k + 1, :]
        y = _layer_norm(acc, lg_ref[...], lb_ref[...])
        y = y * _sigmoid(y)
        o_ref[r0:r0 + CONV_ROWS, :] = _dot(y.astype(BF16), wpw_ref[...])

    @pl.when(t == nt - 1)
    def _():
        nbuf_ref[...] = zp_ref[tt + off:tt + CONV_PAD, :]


def _conv(p_all, row0, buf0, l, conv_w, conv_b, ln_g, ln_b, w_pw, *, nb, t_len):
    tt = min(t_len, 512)
    nt = t_len // tt
    blk0 = row0 // tt
    par = lambda shape: pl.BlockSpec((None,) + shape, lambda b, t: (l,) + (0,) * len(shape))
    buf_spec = pl.BlockSpec((None, CONV_K - 1, 256), lambda b, t: (b, 0, 0))
    return pl.pallas_call(
        functools.partial(_conv_body, tt=tt, nt=nt),
        grid=(nb, nt),
        in_specs=[
            pl.BlockSpec((tt, 512), lambda b, t: (blk0 + b * nt + t, 0)),
            buf_spec,
            par((CONV_K, 256)), par((1, 256)), par((1, 256)), par((1, 256)), par((256, 256)),
        ],
        out_specs=[pl.BlockSpec((tt, 256), lambda b, t: (b * nt + t, 0)), buf_spec],
        out_shape=[jax.ShapeDtypeStruct((nb * t_len, 256), F32),
                   jax.ShapeDtypeStruct((nb, CONV_K - 1, 256), F32)],
        scratch_shapes=[pltpu.VMEM((CONV_PAD + tt, 256), F32)],
        name="conformer_conv",
        compiler_params=_cparams("parallel", "arbitrary"),
    )(p_all, buf0, conv_w, conv_b, ln_g, ln_b, w_pw)


def _gmlp_body(p_ref, lg_ref, lb_ref, ws_ref, bs_ref, o_ref, v_ref, *, lch):
    z = _gelu(p_ref[...])
    u = z[:, 0:256]
    v = _layer_norm(z[:, 256:512], lg_ref[...], lb_ref[...])
    v_ref[...] = v
    vb = v.astype(BF16)
    row = lax.broadcasted_iota(jnp.int32, (lch, lch), 0)
    col = lax.broadcasted_iota(jnp.int32, (lch, lch), 1)
    lane = lax.broadcasted_iota(jnp.int32, (lch, 256), 1)
    mixed = None
    for h in range(GMLP_HEADS):
        w = jnp.where(row >= col, ws_ref[h, 0:lch, 0:lch], 0.0).astype(BF16)
        mh = _dot(w, vb)
        mixed = mh if mixed is None else jnp.where(lane >= h * 64, mh, mixed)
    o_ref[...] = u * (mixed + bs_ref[0:lch, :])


def _gmlp(p_all, row0, l, ln_g, ln_b, ws, bs_wide, *, nb, t_len):
    lch = min(t_len, GMLP_CHUNK)
    nc = t_len // lch
    blk0 = row0 // lch
    par = lambda shape: pl.BlockSpec((None,) + shape, lambda b: (l,) + (0,) * len(shape))
    return pl.pallas_call(
        functools.partial(_gmlp_body, lch=lch),
        grid=(nb * nc,),
        in_specs=[
            pl.BlockSpec((lch, 512), lambda b: (blk0 + b, 0)),
            par((1, 256)), par((1, 256)), par((GMLP_HEADS, GMLP_CHUNK, GMLP_CHUNK)),
            par((GMLP_CHUNK, 256)),
        ],
        out_specs=[pl.BlockSpec((lch, 256), lambda b: (b, 0)), pl.BlockSpec((lch, 256), lambda b: (b, 0))],
        out_shape=[jax.ShapeDtypeStruct((nb * t_len, 256), F32)] * 2,
        name="gmlp",
        compiler_params=_cparams("parallel"),
    )(p_all, ln_g, ln_b, ws, bs_wide)


def _memkv_body(m_ref, g_ref, wk_ref, wv_ref, k_ref, v_ref):
    h = _rms(m_ref[...], g_ref[...]).astype(BF16)
    k_ref[...] = _dot(h, wk_ref[...])
    v_ref[...] = _dot(h, wv_ref[...])


def _memkv(mem, l, norm, wk, wv):
    nb = mem.shape[0]
    par = lambda shape: pl.BlockSpec((None,) + shape, lambda b: (l,) + (0,) * len(shape))
    io = pl.BlockSpec((None, N_MEM, X_WIDTH), lambda b: (b, 0, 0))
    return pl.pallas_call(
        _memkv_body,
        grid=(nb,),
        in_specs=[pl.BlockSpec((None, N_MEM, D_MODEL), lambda b: (b, 0, 0)),
                  par((1, D_MODEL)), par((D_MODEL, X_WIDTH)), par((D_MODEL, X_WIDTH))],
        out_specs=[io, io],
        out_shape=[jax.ShapeDtypeStruct((nb, N_MEM, X_WIDTH), F32)] * 2,
        name="mem_kv",
        compiler_params=_cparams("parallel"),
    )(mem, norm, wk, wv)


MIX_WIDTHS = (128, 128, 256, 256, 256)


def _mix_xattn_body(x_ref, a0_ref, a1_ref, b_ref, c_ref, d_ref, wout_ref, g_ref, wq_ref, wo_ref, mk_ref, mv_ref,
                    o_ref, oh_ref, *, nbs, tpb):
    x = x_ref[...]
    off = 0
    for width, m_ref in zip(MIX_WIDTHS, (a0_ref, a1_ref, b_ref, c_ref, d_ref)):
        x = x + _dot(m_ref[...].astype(BF16), wout_ref[off:off + width, :])
        off += width
    h = _rms(x, g_ref[...]).astype(BF16)
    q = (_dot(h, wq_ref[...]) * (X_HEAD_DIM ** -0.5)).astype(BF16)
    for bi in range(nbs):
        r0 = bi * tpb
        for hd in range(X_HEADS):
            c0 = hd * X_HEAD_DIM
            kh = mk_ref[bi, :, c0:c0 + X_HEAD_DIM].astype(BF16)
            vh = mv_ref[bi, :, c0:c0 + X_HEAD_DIM].astype(BF16)
            s = _dot_nt(q[r0:r0 + tpb, c0:c0 + X_HEAD_DIM], kh)
            p = jnp.exp(s - jnp.max(s, axis=-1, keepdims=True))
            o = _dot(p.astype(BF16), vh) * (1.0 / jnp.sum(p, axis=-1, keepdims=True))
            oh_ref[r0:r0 + tpb, c0:c0 + X_HEAD_DIM] = o.astype(BF16)
    o_ref[...] = x + _dot(oh_ref[...], wo_ref[...])


def _mix_xattn(x_all, row0, mix, l, w_out, norm, wq, wo, mk, mv, lmem, *, nb, t_len, tm):
    tpb = min(t_len, tm)
    nbs = tm // tpb
    spb = t_len // tpb
    steps = nb * t_len // tm
    blk0 = row0 // tm
    par = lambda shape: pl.BlockSpec((None,) + shape, lambda i: (l,) + (0,) * len(shape))
    mem_spec = pl.BlockSpec((None, nbs, N_MEM, X_WIDTH), lambda i: (lmem, i // spb, 0, 0))
    x_spec = pl.BlockSpec((tm, D_MODEL), lambda i: (blk0 + i, 0))
    return pl.pallas_call(
        functools.partial(_mix_xattn_body, nbs=nbs, tpb=tpb),
        grid=(steps,),
        in_specs=[x_spec] + [pl.BlockSpec((tm, w), lambda i: (i, 0)) for w in MIX_WIDTHS] + [
            par((D_MODEL, D_MODEL)), par((1, D_MODEL)), par((D_MODEL, X_WIDTH)), par((X_WIDTH, D_MODEL)),
            mem_spec, mem_spec],
        out_specs=x_spec,
        out_shape=jax.ShapeDtypeStruct(x_all.shape, F32),
        scratch_shapes=[pltpu.VMEM((tm, X_WIDTH), BF16)],
        input_output_aliases={0: 0},
        name="mix_xattn",
        compiler_params=_cparams("arbitrary"),
    )(x_all, *mix, w_out, norm, wq, wo, mk, mv)


def _token_tile(n, cap):
    t = cap
    while n % t:
        t //= 2
    return t


def kernel(x_prompt, x_sample, mem_prompt, cache_attn_k, cache_attn_v, state_ssm_re, state_ssm_im, state_conv, cache_mem_k, cache_mem_v, ffn1_norm, ffn1_w_gate, ffn1_w_up, ffn1_w_down, mix_norm, w_in, w_out, ssm_a_re, ssm_a_im, ssm_b_re, ssm_b_im, ssm_c_re, ssm_c_im, ssm_d, ssm_log_dt, ssm_w_glu, ssm_b_glu, dattn_lq1, dattn_lk1, dattn_lq2, dattn_lk2, dattn_norm, conv_w, conv_b, conv_ln_g, conv_ln_b, conv_w_pw, gmlp_ln_g, gmlp_ln_b, gmlp_ws, gmlp_bs, xattn_norm, mem_norm, xattn_wq, xattn_wk, xattn_wv, xattn_wo, ffn2_norm, ffn2_w_gate, ffn2_w_up, ffn2_w_down, final_norm):
    depth = w_in.shape[0]
    bp, tp, _ = x_prompt.shape
    bs, ts, _ = x_sample.shape
    past = cache_attn_k.shape[2]
    n_p, n_s = bp * tp, bs * ts
    n = n_p + n_s
    tm = _token_tile(math.gcd(n_p, n_s), 1024)
    tm_x = _token_tile(tp, 1024)

    bf = lambda w: w.astype(BF16)
    row = lambda v: v.reshape(depth, 1, -1)
    ffn1 = (row(ffn1_norm), bf(ffn1_w_gate), bf(ffn1_w_up), bf(ffn1_w_down))
    ffn2 = (row(ffn2_norm), bf(ffn2_w_gate), bf(ffn2_w_up), bf(ffn2_w_down))
    w_in_b, w_out_b = bf(w_in), bf(w_out)
    wq_b, wk_b, wv_b, wo_b = bf(xattn_wq), bf(xattn_wk), bf(xattn_wv), bf(xattn_wo)
    wpw_b = bf(conv_w_pw)
    lq = jnp.stack([dattn_lq1, dattn_lq2], axis=1)
    lk = jnp.stack([dattn_lk1, dattn_lk2], axis=1)
    gh = jnp.tile(dattn_norm, (1, LANES // ATT_HEAD_DIM)).reshape(depth, 1, LANES)
    bs_wide = jnp.repeat(jnp.swapaxes(gmlp_bs, 1, 2), 256 // GMLP_HEADS, axis=2)
    final_g = final_norm.reshape(1, D_MODEL)
    cache_k = cache_attn_k.reshape(depth, bs, past, 256)
    cache_v = cache_attn_v.reshape(depth, bs, past, 256)
    mem_k_s = cache_mem_k.reshape(depth, bs, N_MEM, X_WIDTH)
    mem_v_s = cache_mem_v.reshape(depth, bs, N_MEM, X_WIDTH)
    zero_state = jnp.zeros((bp, 2 * SSM_COLS), F32)
    zero_buf = jnp.zeros((bp, CONV_K - 1, 256), F32)

    x = jnp.concatenate([x_prompt.reshape(n_p, D_MODEL), x_sample.reshape(n_s, D_MODEL)], axis=0)
    outs = {name: [] for name in ("pk", "pv", "psre", "psim", "pconv", "pmk", "pmv",
                                  "sk", "sv", "ssre", "ssim", "sconv", "sgv")}
    for l in range(depth):
        lam_init = 0.8 - 0.6 * math.exp(-0.3 * l)
        x = _ffn(x, l, *ffn1, final_g, final=False, tm=tm, tf=512)
        u0, u1, q, k, v, pc, pg = _proj_in(x, l, row(mix_norm), w_in_b, tm=tm)
        s5p = _s5_params(ssm_a_re[l], ssm_a_im[l], ssm_b_re[l], ssm_b_im[l], ssm_c_re[l], ssm_c_im[l],
                         ssm_d[l], ssm_log_dt[l], ssm_w_glu[l], ssm_b_glu[l])
        s0_s = jnp.concatenate([state_ssm_re[l].reshape(bs, SSM_COLS), state_ssm_im[l].reshape(bs, SSM_COLS)], axis=1)
        a_p, st_p = _s5(u0, u1, 0, zero_state, s5p, nb=bp, t_len=tp, chained=True)
        b_p = _attn_prompt(q, k, v, l, lq, lk, gh, lam_init, nb=bp, t_len=tp, tq=min(tp, 256), tk=min(tp, 512))
        c_p, buf_p = _conv(pc, 0, zero_buf, l, conv_w, row(conv_b), row(conv_ln_g), row(conv_ln_b), wpw_b,
                           nb=bp, t_len=tp)
        d_p, _ = _gmlp(pg, 0, l, row(gmlp_ln_g), row(gmlp_ln_b), gmlp_ws, bs_wide, nb=bp, t_len=tp)
        mk_p, mv_p = _memkv(mem_prompt, l, row(mem_norm), wk_b, wv_b)
        a_s, st_s = _s5(u0, u1, n_p, s0_s, s5p, nb=bs, t_len=ts, chained=False)
        b_s = _attn_sample(q, k, v, n_p, cache_k, cache_v, l, lq, lk, gh, lam_init, nb=bs, t_len=ts)
        c_s, buf_s = _conv(pc, n_p, state_conv[l], l, conv_w, row(conv_b), row(conv_ln_g), row(conv_ln_b), wpw_b,
                           nb=bs, t_len=ts)
        d_s, gv_s = _gmlp(pg, n_p, l, row(gmlp_ln_g), row(gmlp_ln_b), gmlp_ws, bs_wide, nb=bs, t_len=ts)
        x = _mix_xattn(x, 0, (*a_p, b_p, c_p, d_p), l, w_out_b, row(xattn_norm), wq_b, wo_b, mk_p[None], mv_p[None], 0,
                       nb=bp, t_len=tp, tm=tm_x)
        x = _mix_xattn(x, n_p, (*a_s, b_s, c_s, d_s), l, w_out_b, row(xattn_norm), wq_b, wo_b,
                       mem_k_s, mem_v_s, l, nb=bs, t_len=ts, tm=_token_tile(n_s, 512))
        x = _ffn(x, l, *ffn2, final_g, final=(l == depth - 1), tm=tm, tf=512)

        outs["pk"].append(k[:n_p].reshape(bp, tp, ATT_HEADS, ATT_HEAD_DIM))
        outs["pv"].append(v[:n_p].reshape(bp, tp, ATT_HEADS, ATT_HEAD_DIM))
        outs["psre"].append(st_p[:, :SSM_COLS].reshape(bp, SSM_GROUPS, SSM_STATE))
        outs["psim"].append(st_p[:, SSM_COLS:].reshape(bp, SSM_GROUPS, SSM_STATE))
        outs["pconv"].append(buf_p)
        outs["pmk"].append(mk_p.reshape(bp, N_MEM, X_HEADS, X_HEAD_DIM))
        outs["pmv"].append(mv_p.reshape(bp, N_MEM, X_HEADS, X_HEAD_DIM))
        outs["sk"].append(k[n_p:].reshape(bs, ts, ATT_HEADS, ATT_HEAD_DIM))
        outs["sv"].append(v[n_p:].reshape(bs, ts, ATT_HEADS, ATT_HEAD_DIM))
        outs["ssre"].append(st_s[:, :SSM_COLS].reshape(bs, SSM_GROUPS, SSM_STATE))
        outs["ssim"].append(st_s[:, SSM_COLS:].reshape(bs, SSM_GROUPS, SSM_STATE))
        outs["sconv"].append(buf_s)
        outs["sgv"].append(gv_s.reshape(bs, ts, 256))

    st = {name: jnp.stack(vals) for name, vals in outs.items()}
    return (x[:n_p].reshape(bp, tp, D_MODEL), x[n_p:].reshape(bs, ts, D_MODEL),
            st["pk"], st["pv"], st["psre"], st["psim"], st["pconv"], st["pmk"], st["pmv"],
            st["sk"], st["sv"], st["ssre"], st["ssim"], st["sconv"], st["sgv"])
```
